```python
import jax, jax.numpy as jnp
from jax import lax
import numpy as np

D_MODEL = 1024
BATCH = 16
SEQ = 4096
DEPTH = 4
DEC_BATCH = 16
DEC_SEQ = 64
PAST_LEN = 1024

CHUNK = 64
QBLOCK = 128
N_MIXERS = 4
EPS = 1e-6
NEG = -1e30
PLE_DIM = 256
FFN_HIDDEN = -(-8 * D_MODEL // (3 * 256)) * 256
SB_HEADS = 16
SB_HEAD_DIM = D_MODEL // SB_HEADS
MLA_HEADS = 16
MLA_Q_RANK = 256
MLA_KV_RANK = 128
MLA_NOPE = 64
MLA_ROPE = 32
MLA_V = 64
MLA_SCALE = (MLA_NOPE + MLA_ROPE) ** -0.5
ROPE_THETA = 10000.0
CONV_WIDTH = 31
CONV_HIST = CONV_WIDTH - 1
BAND_HEADS = 16
BAND_HEAD_DIM = D_MODEL // BAND_HEADS
BAND_CHUNKS = 8
BAND_PAST = BAND_CHUNKS * CHUNK
BAND_LEN = BAND_PAST + CHUNK
REL_CLIP = 128

kernel_name = 'hybrid_streaming_encoder_step'

F32 = jnp.float32


def _type_layers(m):
    return len(range(m, DEPTH, N_MIXERS))


def _rmsnorm(x, g):
    xf = x.astype(F32)
    y = xf * lax.rsqrt(jnp.mean(xf * xf, axis=-1, keepdims=True) + EPS)
    return (y * g.astype(F32)).astype(x.dtype)


def _layernorm(x, g, b):
    xf = x.astype(F32)
    xc = xf - jnp.mean(xf, axis=-1, keepdims=True)
    var = jnp.mean(xc * xc, axis=-1, keepdims=True)
    return (xc * lax.rsqrt(var + EPS) * g.astype(F32) + b.astype(F32)).astype(x.dtype)


def _heads_qkv(h, w, n_heads, head_dim):
    b, s, _ = h.shape
    qkv = (h @ w).reshape(b, s, 3, n_heads, head_dim)
    return qkv[:, :, 0], qkv[:, :, 1], qkv[:, :, 2]


def _to_blocks(t):
    b, s, h, d = t.shape
    return t.reshape(b, s // QBLOCK, QBLOCK, h, d).transpose(1, 0, 3, 2, 4)


def _from_blocks(t):
    nb, b, h, qb, d = t.shape
    return t.transpose(1, 0, 3, 2, 4).reshape(b, nb * qb, h * d)


def _sb_block(qf, k, v, qpos, kpos, acc_log):
    z = jnp.einsum('bhqd,bhkd->bhqk', qf, k.astype(F32))
    valid = kpos[None, :] < qpos[:, None]
    log_keep = jnp.where(valid, jax.nn.log_sigmoid(-z), 0.0)
    rc = lax.cumsum(log_keep, axis=3, reverse=True)
    log_w = jax.nn.log_sigmoid(z) + (rc - log_keep) + acc_log[..., None]
    w = jnp.where(valid, jnp.exp(log_w), 0.0)
    out = jnp.einsum('bhqk,bhkd->bhqd', w, v.astype(F32))
    return out, acc_log + rc[..., 0]


def _sb_attend_prompt(q, k, v):
    b, s, h, d = q.shape
    qb, kb, vb = _to_blocks(q), _to_blocks(k), _to_blocks(v)
    scale = d ** -0.5
    ar = jnp.arange(QBLOCK, dtype=jnp.int32)

    def per_query_block(args):
        qi, qidx = args
        qf = qi.astype(F32) * scale
        qpos = qidx * QBLOCK + ar

        def body(it, carry):
            acc_log, out = carry
            j = qidx - it
            kj = lax.dynamic_index_in_dim(kb, j, 0, keepdims=False)
            vj = lax.dynamic_index_in_dim(vb, j, 0, keepdims=False)
            o, acc_log = _sb_block(qf, kj, vj, qpos, j * QBLOCK + ar, acc_log)
            return acc_log, out + o

        init = (jnp.zeros((b, h, QBLOCK), F32), jnp.zeros((b, h, QBLOCK, d), F32))
        return lax.fori_loop(0, qidx + 1, body, init)[1]

    o = lax.map(per_query_block, (qb, jnp.arange(s // QBLOCK, dtype=jnp.int32)))
    return _from_blocks(o).astype(q.dtype)


def _sb_attend_dense(q, k, v, qpos, kpos):
    b, t, h, d = q.shape
    qf = q.transpose(0, 2, 1, 3).astype(F32) * d ** -0.5
    o, _ = _sb_block(qf, k.transpose(0, 2, 1, 3), v.transpose(0, 2, 1, 3), qpos, kpos,
                     jnp.zeros((b, h, t), F32))
    return o.transpose(0, 2, 1, 3).reshape(b, t, h * d).astype(q.dtype)


def _softmax_attend_prompt(q, k, v, scale):
    b, s, h, _ = q.shape
    dv = v.shape[-1]
    qb, kb, vb = _to_blocks(q), _to_blocks(k), _to_blocks(v)
    ar = jnp.arange(QBLOCK, dtype=jnp.int32)

    def per_query_block(args):
        qi, qidx = args
        qf = qi.astype(F32) * scale
        qchunk = (qidx * QBLOCK + ar) // CHUNK

        def body(j, carry):
            m, l, acc = carry
            kj = lax.dynamic_index_in_dim(kb, j, 0, keepdims=False)
            vj = lax.dynamic_index_in_dim(vb, j, 0, keepdims=False)
            sc = jnp.einsum('bhqd,bhkd->bhqk', qf, kj.astype(F32))
            vis = ((j * QBLOCK + ar) // CHUNK)[None, :] <= qchunk[:, None]
            sc = jnp.where(vis, sc, NEG)
            m_new = jnp.maximum(m, jnp.max(sc, axis=-1))
            p = jnp.exp(sc - m_new[..., None])
            corr = jnp.exp(m - m_new)
            acc = acc * corr[..., None] + jnp.einsum('bhqk,bhkd->bhqd', p, vj.astype(F32))
            return m_new, l * corr + jnp.sum(p, axis=-1), acc

        init = (jnp.full((b, h, QBLOCK), NEG, F32), jnp.zeros((b, h, QBLOCK), F32),
                jnp.zeros((b, h, QBLOCK, dv), F32))
        m, l, acc = lax.fori_loop(0, qidx + 1, body, init)
        return acc / l[..., None]

    o = lax.map(per_query_block, (qb, jnp.arange(s // QBLOCK, dtype=jnp.int32)))
    return _from_blocks(o).astype(q.dtype)


def _softmax_attend_dense(q, k, v, qpos, kpos, scale):
    b, t, h, _ = q.shape
    sc = jnp.einsum('bqhd,bkhd->bhqk', q.astype(F32) * scale, k.astype(F32))
    vis = (kpos[None, :] // CHUNK) <= (qpos[:, None] // CHUNK)
    p = jax.nn.softmax(jnp.where(vis, sc, NEG), axis=-1)
    o = jnp.einsum('bhqk,bkhd->bqhd', p, v.astype(F32))
    return o.reshape(b, t, -1).astype(q.dtype)


def _rope(x, pos):
    half = x.shape[-1] // 2
    inv = ROPE_THETA ** (-jnp.arange(half, dtype=F32) / half)
    ang = pos.astype(F32)[:, None] * inv[None, :]
    cos = jnp.cos(ang)[None, :, None, :]
    sin = jnp.sin(ang)[None, :, None, :]
    xf = x.astype(F32)
    x1, x2 = xf[..., :half], xf[..., half:]
    return jnp.concatenate([x1 * cos - x2 * sin, x2 * cos + x1 * sin], axis=-1).astype(x.dtype)


def _mla_project(h, pos, wdq, gq, wuq, wdkv, gkv):
    b, s, _ = h.shape
    cq = _rmsnorm(h @ wdq, gq)
    q = (cq @ wuq).reshape(b, s, MLA_HEADS, MLA_NOPE + MLA_ROPE)
    q = jnp.concatenate([q[..., :MLA_NOPE], _rope(q[..., MLA_NOPE:], pos)], axis=-1)
    kv = h @ wdkv
    ckv = _rmsnorm(kv[..., :MLA_KV_RANK], gkv)
    kpe = _rope(kv[..., MLA_KV_RANK:][:, :, None, :], pos)[:, :, 0, :]
    return q, ckv, kpe


def _mla_expand(ckv, kpe, wuk, wuv):
    b, s, _ = ckv.shape
    k_nope = (ckv @ wuk).reshape(b, s, MLA_HEADS, MLA_NOPE)
    k = jnp.concatenate([k_nope, jnp.broadcast_to(kpe[:, :, None, :], (b, s, MLA_HEADS, MLA_ROPE))], axis=-1)
    v = (ckv @ wuv).reshape(b, s, MLA_HEADS, MLA_V)
    return k, v


def _conv_module(h, hist, w1, b1, wdw, bdw, ln_g, ln_b, w2, b2):
    u = h @ w1 + b1
    u = u[..., :D_MODEL] * jax.nn.sigmoid(u[..., D_MODEL:])
    up = jnp.concatenate([hist.astype(u.dtype), u], axis=1)
    y = lax.conv_general_dilated(up, wdw[:, None, :].astype(up.dtype), window_strides=(1,),
                                 padding='VALID', dimension_numbers=('NWC', 'WIO', 'NWC'),
                                 feature_group_count=D_MODEL) + bdw
    y = jax.nn.silu(_layernorm(y, ln_g, ln_b))
    return y @ w2 + b2, up[:, -CONV_HIST:]


def _band_block(q, k, v, qpos, kpos, rel_bias):
    d = q.shape[-1]
    sc = jnp.einsum('bqhd,bkhd->bhqk', q.astype(F32) * d ** -0.5, k.astype(F32))
    rel = jnp.clip(qpos[:, None] - kpos[None, :], -REL_CLIP, REL_CLIP) + REL_CLIP
    sc = sc + rel_bias.astype(F32)[:, rel][None]
    qc = qpos // CHUNK
    kc = kpos // CHUNK
    vis = (kpos[None, :] >= 0) & (kc[None, :] <= qc[:, None]) & (qc[:, None] - kc[None, :] <= BAND_CHUNKS)
    p = jax.nn.softmax(jnp.where(vis[None, None], sc, NEG), axis=-1)
    return jnp.einsum('bhqk,bkhd->bqhd', p, v.astype(F32))


def _band_attend_prompt(q, k, v, rel_bias):
    b, s, h, d = q.shape
    nc = s // CHUNK
    pad = jnp.zeros((b, BAND_PAST, h, d), k.dtype)
    kp = jnp.concatenate([pad, k], axis=1)
    vp = jnp.concatenate([pad, v], axis=1)
    qc = q.reshape(b, nc, CHUNK, h, d).transpose(1, 0, 2, 3, 4)
    ar_q = jnp.arange(CHUNK, dtype=jnp.int32)
    ar_k = jnp.arange(BAND_LEN, dtype=jnp.int32)

    def per_chunk(args):
        qi, c = args
        start = c * CHUNK
        kb = lax.dynamic_slice_in_dim(kp, start, BAND_LEN, axis=1)
        vb = lax.dynamic_slice_in_dim(vp, start, BAND_LEN, axis=1)
        return _band_block(qi, kb, vb, start + ar_q, start - BAND_PAST + ar_k, rel_bias)

    o = lax.map(per_chunk, (qc, jnp.arange(nc, dtype=jnp.int32)))
    return o.transpose(1, 0, 2, 3, 4).reshape(b, s, h * d).astype(q.dtype)


def _swiglu(h, w1, w3, w2):
    return (jax.nn.silu(h @ w1) * (h @ w3)) @ w2


def _ple(x, p, g, w_gate, w_proj):
    return jax.nn.sigmoid(_rmsnorm(x, g) @ w_gate) * (p @ w_proj)


def setup_inputs(seed: int = 0) -> dict:
    key = jax.random.key(seed)
    ks = iter(jax.random.split(key, 48))
    D = D_MODEL

    def nrm(shape, scale=1.0):
        return jax.random.normal(next(ks), shape, F32) * scale

    def gain(shape):
        return 1.0 + 0.05 * nrm(shape)

    na, nb, nc, nd = (_type_layers(m) for m in range(N_MIXERS))
    band_rows = min(BAND_PAST, PAST_LEN)
    return {
        'x_prompt': nrm((BATCH, SEQ, D)),
        'x_sample': nrm((DEC_BATCH, DEC_SEQ, D)),
        'p_prompt': nrm((DEPTH, BATCH, SEQ, PLE_DIM)),
        'p_sample': nrm((DEPTH, DEC_BATCH, DEC_SEQ, PLE_DIM)),
        'cache_sb_k': nrm((na, DEC_BATCH, PAST_LEN, SB_HEADS, SB_HEAD_DIM)),
        'cache_sb_v': nrm((na, DEC_BATCH, PAST_LEN, SB_HEADS, SB_HEAD_DIM)),
        'cache_mla_ckv': nrm((nb, DEC_BATCH, PAST_LEN, MLA_KV_RANK)),
        'cache_mla_kpe': nrm((nb, DEC_BATCH, PAST_LEN, MLA_ROPE)),
        'state_conv': nrm((nc, DEC_BATCH, CONV_HIST, D), 0.5),
        'cache_band_k': nrm((nd, DEC_BATCH, band_rows, BAND_HEADS, BAND_HEAD_DIM)),
        'cache_band_v': nrm((nd, DEC_BATCH, band_rows, BAND_HEADS, BAND_HEAD_DIM)),
        'norm_mix': gain((DEPTH, D)),
        'norm_ffn': gain((DEPTH, D)),
        'norm_ple': gain((DEPTH, D)),
        'norm_final': gain((D,)),
        'ffn_w1': nrm((DEPTH, D, FFN_HIDDEN), D ** -0.5),
        'ffn_w3': nrm((DEPTH, D, FFN_HIDDEN), D ** -0.5),
        'ffn_w2': nrm((DEPTH, FFN_HIDDEN, D), FFN_HIDDEN ** -0.5),
        'ple_proj': nrm((DEPTH, PLE_DIM, D), PLE_DIM ** -0.5),
        'ple_gate': nrm((DEPTH, D, D), D ** -0.5),
        'sb_wqkv': nrm((na, D, 3 * SB_HEADS * SB_HEAD_DIM), D ** -0.5),
        'sb_wo': nrm((na, SB_HEADS * SB_HEAD_DIM, D), (SB_HEADS * SB_HEAD_DIM) ** -0.5),
        'mla_wdq': nrm((nb, D, MLA_Q_RANK), D ** -0.5),
        'mla_gq': gain((nb, MLA_Q_RANK)),
        'mla_wuq': nrm((nb, MLA_Q_RANK, MLA_HEADS * (MLA_NOPE + MLA_ROPE)), MLA_Q_RANK ** -0.5),
        'mla_wdkv': nrm((nb, D, MLA_KV_RANK + MLA_ROPE), D ** -0.5),
        'mla_gkv': gain((nb, MLA_KV_RANK)),
        'mla_wuk': nrm((nb, MLA_KV_RANK, MLA_HEADS * MLA_NOPE), MLA_KV_RANK ** -0.5),
        'mla_wuv': nrm((nb, MLA_KV_RANK, MLA_HEADS * MLA_V), MLA_KV_RANK ** -0.5),
        'mla_wo': nrm((nb, MLA_HEADS * MLA_V, D), (MLA_HEADS * MLA_V) ** -0.5),
        'conv_w1': nrm((nc, D, 2 * D), D ** -0.5),
        'conv_b1': nrm((nc, 2 * D), 0.02),
        'conv_dw': nrm((nc, CONV_WIDTH, D), CONV_WIDTH ** -0.5),
        'conv_bdw': nrm((nc, D), 0.02),
        'conv_ln_g': gain((nc, D)),
        'conv_ln_b': nrm((nc, D), 0.02),
        'conv_w2': nrm((nc, D, D), D ** -0.5),
        'conv_b2': nrm((nc, D), 0.02),
        'band_wqkv': nrm((nd, D, 3 * BAND_HEADS * BAND_HEAD_DIM), D ** -0.5),
        'band_rel_bias': nrm((nd, BAND_HEADS, 2 * REL_CLIP + 1), 0.2),
        'band_wo': nrm((nd, BAND_HEADS * BAND_HEAD_DIM, D), (BAND_HEADS * BAND_HEAD_DIM) ** -0.5),
    }


def reference(x_prompt, x_sample, p_prompt, p_sample,
              cache_sb_k, cache_sb_v, cache_mla_ckv, cache_mla_kpe, state_conv, cache_band_k, cache_band_v,
              norm_mix, norm_ffn, norm_ple, norm_final,
              ffn_w1, ffn_w3, ffn_w2, ple_proj, ple_gate,
              sb_wqkv, sb_wo,
              mla_wdq, mla_gq, mla_wuq, mla_wdkv, mla_gkv, mla_wuk, mla_wuv, mla_wo,
              conv_w1, conv_b1, conv_dw, conv_bdw, conv_ln_g, conv_ln_b, conv_w2, conv_b2,
              band_wqkv, band_rel_bias, band_wo):
    xp, xs = x_prompt, x_sample
    s = xp.shape[1]
    t = xs.shape[1]
    pos_p = jnp.arange(s, dtype=jnp.int32)
    pos_s = PAST_LEN + jnp.arange(t, dtype=jnp.int32)

    sb_kp, sb_vp, sb_ks, sb_vs = [], [], [], []
    mla_cp, mla_pp, mla_cs, mla_ps = [], [], [], []
    conv_p, conv_s = [], []
    band_kp, band_vp, band_ks, band_vs = [], [], [], []

    for i in range(DEPTH):
        m, j = i % N_MIXERS, i // N_MIXERS
        hp = _rmsnorm(xp, norm_mix[i])
        hs = _rmsnorm(xs, norm_mix[i])
        if m == 0:
            q, k, v = _heads_qkv(hp, sb_wqkv[j], SB_HEADS, SB_HEAD_DIM)
            mp = _sb_attend_prompt(q, k, v) @ sb_wo[j]
            sb_kp.append(k)
            sb_vp.append(v)
            q, k, v = _heads_qkv(hs, sb_wqkv[j], SB_HEADS, SB_HEAD_DIM)
            kk = jnp.concatenate([cache_sb_k[j].astype(k.dtype), k], axis=1)
            vv = jnp.concatenate([cache_sb_v[j].astype(v.dtype), v], axis=1)
            kpos = jnp.arange(kk.shape[1], dtype=jnp.int32)
            ms = _sb_attend_dense(q, kk, vv, pos_s, kpos) @ sb_wo[j]
            sb_ks.append(k)
            sb_vs.append(v)
        elif m == 1:
            q, ckv, kpe = _mla_project(hp, pos_p, mla_wdq[j], mla_gq[j], mla_wuq[j], mla_wdkv[j], mla_gkv[j])
            k, v = _mla_expand(ckv, kpe, mla_wuk[j], mla_wuv[j])
            mp = _softmax_attend_prompt(q, k, v, MLA_SCALE) @ mla_wo[j]
            mla_cp.append(ckv)
            mla_pp.append(kpe)
            q, ckv, kpe = _mla_project(hs, pos_s, mla_wdq[j], mla_gq[j], mla_wuq[j], mla_wdkv[j], mla_gkv[j])
            ckv_all = jnp.concatenate([cache_mla_ckv[j].astype(ckv.dtype), ckv], axis=1)
            kpe_all = jnp.concatenate([cache_mla_kpe[j].astype(kpe.dtype), kpe], axis=1)
            k, v = _mla_expand(ckv_all, kpe_all, mla_wuk[j], mla_wuv[j])
            kpos = jnp.arange(k.shape[1], dtype=jnp.int32)
            ms = _softmax_attend_dense(q, k, v, pos_s, kpos, MLA_SCALE) @ mla_wo[j]
            mla_cs.append(ckv)
            mla_ps.append(kpe)
        elif m == 2:
            hist0 = jnp.zeros((hp.shape[0], CONV_HIST, D_MODEL), hp.dtype)
            mp, st = _conv_module(hp, hist0, conv_w1[j], conv_b1[j], conv_dw[j], conv_bdw[j],
                                  conv_ln_g[j], conv_ln_b[j], conv_w2[j], conv_b2[j])
            conv_p.append(st)
            ms, st = _conv_module(hs, state_conv[j], conv_w1[j], conv_b1[j], conv_dw[j], conv_bdw[j],
                                  conv_ln_g[j], conv_ln_b[j], conv_w2[j], conv_b2[j])
            conv_s.append(st)
        else:
            q, k, v = _heads_qkv(hp, band_wqkv[j], BAND_HEADS, BAND_HEAD_DIM)
            mp = _band_attend_prompt(q, k, v, band_rel_bias[j]) @ band_wo[j]
            keep = min(BAND_PAST, s)
            band_kp.append(k[:, s - keep:])
            band_vp.append(v[:, s - keep:])
            q, k, v = _heads_qkv(hs, band_wqkv[j], BAND_HEADS, BAND_HEAD_DIM)
            n_past = cache_band_k.shape[2]
            kk = jnp.concatenate([cache_band_k[j].astype(k.dtype), k], axis=1)
            vv = jnp.concatenate([cache_band_v[j].astype(v.dtype), v], axis=1)
            kpos = PAST_LEN - n_past + jnp.arange(n_past + t, dtype=jnp.int32)
            ob = _band_block(q, kk, vv, pos_s, kpos, band_rel_bias[j])
            ms = ob.reshape(q.shape[0], t, -1).astype(q.dtype) @ band_wo[j]
            band_ks.append(kk[:, t:])
            band_vs.append(vv[:, t:])
        xp = xp + mp
        xs = xs + ms
        xp = xp + _swiglu(_rmsnorm(xp, norm_ffn[i]), ffn_w1[i], ffn_w3[i], ffn_w2[i])
        xs = xs + _swiglu(_rmsnorm(xs, norm_ffn[i]), ffn_w1[i], ffn_w3[i], ffn_w2[i])
        xp = xp + _ple(xp, p_prompt[i], norm_ple[i], ple_gate[i], ple_proj[i])
        xs = xs + _ple(xs, p_sample[i], norm_ple[i], ple_gate[i], ple_proj[i])

    y_prompt = _rmsnorm(xp, norm_final)
    y_sample = _rmsnorm(xs, norm_final)
    return (y_prompt, y_sample,
            jnp.stack(sb_kp), jnp.stack(sb_vp), jnp.stack(sb_ks), jnp.stack(sb_vs),
            jnp.stack(mla_cp), jnp.stack(mla_pp), jnp.stack(mla_cs), jnp.stack(mla_ps),
            jnp.stack(conv_p), jnp.stack(conv_s),
            jnp.stack(band_kp), jnp.stack(band_vp), jnp.stack(band_ks), jnp.stack(band_vs))
```

```python
import functools
import math

import jax
import jax.numpy as jnp
from jax import lax
from jax.experimental import pallas as pl
from jax.experimental.pallas import tpu as pltpu

F32 = jnp.float32
BF16 = jnp.bfloat16

D = 1024
EPS = 1e-6
NEG = -1e30
HEADS = 16
HEAD_DIM = 64
CHUNK = 64
PLE_DIM = 256
FFN_HIDDEN = 2816
FFN_CHUNK = 256
N_FFN_CHUNKS = FFN_HIDDEN // FFN_CHUNK
MLA_Q_RANK = 256
MLA_KV_RANK = 128
MLA_NOPE = 64
MLA_ROPE = 32
MLA_SCALE = (MLA_NOPE + MLA_ROPE) ** -0.5
ROPE_THETA = 10000.0
CONV_WIDTH = 31
CONV_HIST = CONV_WIDTH - 1
HIST_PAD = 32
BAND_CHUNKS = 8
BAND_PAST = BAND_CHUNKS * CHUNK
REL_CLIP = 128
N_MIXERS = 4

LANES = 128
KEY_TILE = 256
ROW_TILE = 512
VMEM_LIMIT = 56 * 1024 * 1024


def _dot(a, b):
    return jnp.dot(a, b, preferred_element_type=F32)


def _dot_nt(a, b):
    return lax.dot_general(a, b, (((1,), (1,)), ((), ())), preferred_element_type=F32)


def _rms(x, g):
    return x * lax.rsqrt(jnp.mean(x * x, axis=-1, keepdims=True) + EPS) * g


def _sigmoid(x):
    return 1.0 / (1.0 + jnp.exp(-x))


def _params(sem):
    return pltpu.CompilerParams(dimension_semantics=sem, vmem_limit_bytes=VMEM_LIMIT)


def _resident(shape):
    zeros = (0,) * len(shape)
    return pl.BlockSpec(shape, lambda *_: zeros, pipeline_mode=pl.Buffered(1))


def _row_tile(rows):
    return min(ROW_TILE, rows)


def _chunk_of(idx):
    return lax.shift_right_arithmetic(idx, CHUNK.bit_length() - 1)


def _head_lanes(lane, h):
    return lane < HEAD_DIM if h == 0 else lane >= HEAD_DIM


def _qkv_body(x_ref, g_ref, w_ref, q_ref, kf_ref, vf_ref, kb_ref, vb_ref, *, scale):
    h = _rms(x_ref[...], g_ref[...]).astype(BF16)
    q_ref[...] = (_dot(h, w_ref[:, 0:D]) * scale).astype(BF16)
    k = _dot(h, w_ref[:, D:2 * D])
    kf_ref[...] = k
    kb_ref[...] = k.astype(BF16)
    v = _dot(h, w_ref[:, 2 * D:3 * D])
    vf_ref[...] = v
    vb_ref[...] = v.astype(BF16)


def _qkv(x, g, w, scale):
    rows = x.shape[0]
    tm = _row_tile(rows)
    row = lambda i: (i, 0)
    out_sds = lambda dt: jax.ShapeDtypeStruct((rows, D), dt)
    return pl.pallas_call(
        functools.partial(_qkv_body, scale=scale),
        grid=(rows // tm,),
        in_specs=[pl.BlockSpec((tm, D), row), _resident((1, D)), _resident((D, 3 * D))],
        out_specs=[pl.BlockSpec((tm, D), row)] * 5,
        out_shape=[out_sds(BF16), out_sds(F32), out_sds(F32), out_sds(BF16), out_sds(BF16)],
        compiler_params=_params(("parallel",)),
        name="qkv_proj",
    )(x, g.reshape(1, D), w)


def _sb_attn_body(q_ref, k_ref, v_ref, tri_ref, o_ref, *, tq, q_off):
    q0 = (pl.program_id(2) + q_off) * tq
    first_diag = lax.div(q0, KEY_TILE)
    n_diag = max(1, tq // KEY_TILE)
    lane = lax.broadcasted_iota(jnp.int32, (tq, LANES), 1)
    col_minus_row = (lax.broadcasted_iota(jnp.int32, (tq, KEY_TILE), 1)
                     - lax.broadcasted_iota(jnp.int32, (tq, KEY_TILE), 0))
    q_pair = q_ref[0]
    tri = tri_ref[...]
    outs = []
    for h in range(2):
        qh = jnp.where(_head_lanes(lane, h), q_pair, jnp.zeros_like(q_pair))

        def tile(j, carry, masked, qh=qh):
            acc_log, out = carry
            k0 = pl.multiple_of(j * KEY_TILE, KEY_TILE)
            z = _dot_nt(qh, k_ref[0, pl.ds(k0, KEY_TILE), :])
            softplus = jnp.maximum(z, 0.0) + jnp.log(1.0 + jnp.exp(-jnp.abs(z)))
            log_keep = -softplus
            if masked:
                valid = col_minus_row < (q0 - k0)
                log_keep = jnp.where(valid, log_keep, 0.0)
            hi = log_keep.astype(BF16)
            lo = (log_keep - hi.astype(F32)).astype(BF16)
            later = _dot(hi, tri) + _dot(lo, tri)
            w = jnp.exp((z - softplus) + later)
            if masked:
                w = jnp.where(valid, w, 0.0)
            o = _dot(w.astype(BF16), v_ref[0, pl.ds(k0, KEY_TILE), :])
            out = out + jnp.exp(acc_log) * o
            acc_log = acc_log + later[:, 0:1] + log_keep[:, 0:1]
            return acc_log, out

        carry = (jnp.zeros((tq, 1), F32), jnp.zeros((tq, LANES), F32))
        for d in reversed(range(n_diag)):
            carry = tile(first_diag + d, carry, True)
        carry = lax.fori_loop(
            0, first_diag, lambda t, c: tile(first_diag - 1 - t, c, False), carry)
        outs.append(carry[1])
    o_ref[0] = jnp.where(lane < HEAD_DIM, outs[0], outs[1]).astype(BF16)


def _tri():
    j = lax.broadcasted_iota(jnp.int32, (KEY_TILE, KEY_TILE), 0)
    s = lax.broadcasted_iota(jnp.int32, (KEY_TILE, KEY_TILE), 1)
    return (j > s).astype(BF16)


def _sb_attn(q, k, v, *, tq, q_off):
    b, sq, _ = q.shape
    sk = k.shape[1]
    return pl.pallas_call(
        functools.partial(_sb_attn_body, tq=tq, q_off=q_off),
        grid=(b, HEADS // 2, sq // tq),
        in_specs=[pl.BlockSpec((1, tq, LANES), lambda bi, p, qi: (bi, qi, p)),
                  pl.BlockSpec((1, sk, LANES), lambda bi, p, qi: (bi, 0, p)),
                  pl.BlockSpec((1, sk, LANES), lambda bi, p, qi: (bi, 0, p)),
                  pl.BlockSpec((KEY_TILE, KEY_TILE), lambda bi, p, qi: (0, 0))],
        out_specs=pl.BlockSpec((1, tq, LANES), lambda bi, p, qi: (bi, qi, p)),
        out_shape=jax.ShapeDtypeStruct((b, sq, D), BF16),
        compiler_params=_params(("parallel", "parallel", "arbitrary")),
        name="sb_attn",
    )(q, k, v, _tri())


def _value_with_ones(v_tile, h):
    lane = lax.broadcasted_iota(jnp.int32, v_tile.shape, 1)
    return jnp.where(_head_lanes(lane, h), v_tile, jnp.ones_like(v_tile))


def _normalise_pair(accs):
    res = [a * pltpu.roll(1.0 / a, HEAD_DIM, 1) for a in accs]
    lane = lax.broadcasted_iota(jnp.int32, res[0].shape, 1)
    return jnp.where(lane < HEAD_DIM, res[0], res[1])


def _softmax_attn_body(q_ref, k_ref, v_ref, o_ref, *, tq, q_off):
    q0 = (pl.program_id(2) + q_off) * tq
    first_diag = lax.div(q0, KEY_TILE)
    n_diag = max(1, tq // KEY_TILE)
    chunk_diff = (_chunk_of(lax.broadcasted_iota(jnp.int32, (tq, KEY_TILE), 1))
                  - _chunk_of(lax.broadcasted_iota(jnp.int32, (tq, KEY_TILE), 0)))
    accs = []
    for h in range(2):
        qh = q_ref[0, :, LANES * h:LANES * (h + 1)]

        def tile(j, carry, masked, qh=qh, h=h):
            m, acc = carry
            k0 = pl.multiple_of(j * KEY_TILE, KEY_TILE)
            s = _dot_nt(qh, k_ref[0, pl.ds(k0, KEY_TILE), LANES * h:LANES * (h + 1)])
            if masked:
                s = jnp.where(chunk_diff <= _chunk_of(q0 - k0), s, NEG)
            m_new = jnp.maximum(m, jnp.max(s, axis=-1, keepdims=True))
            p = jnp.exp(s - m_new)
            v_ext = _value_with_ones(v_ref[0, pl.ds(k0, KEY_TILE), :], h)
            acc = acc * jnp.exp(m - m_new) + _dot(p.astype(BF16), v_ext)
            return m_new, acc

        carry = (jnp.full((tq, 1), NEG, F32), jnp.zeros((tq, LANES), F32))
        carry = lax.fori_loop(0, first_diag, lambda t, c: tile(t, c, False), carry)
        for d in range(n_diag):
            carry = tile(first_diag + d, carry, True)
        accs.append(carry[1])
    o_ref[0] = _normalise_pair(accs).astype(BF16)


def _softmax_attn(q, k, v, *, tq, q_off):
    b, sq, _ = q.shape
    sk = k.shape[1]
    return pl.pallas_call(
        functools.partial(_softmax_attn_body, tq=tq, q_off=q_off),
        grid=(b, HEADS // 2, sq // tq),
        in_specs=[pl.BlockSpec((1, tq, 2 * LANES), lambda bi, p, qi: (bi, qi, p)),
                  pl.BlockSpec((1, sk, 2 * LANES), lambda bi, p, qi: (bi, 0, p)),
                  pl.BlockSpec((1, sk, LANES), lambda bi, p, qi: (bi, 0, p))],
        out_specs=pl.BlockSpec((1, tq, LANES), lambda bi, p, qi: (bi, qi, p)),
        out_shape=jax.ShapeDtypeStruct((b, sq, D), BF16),
        compiler_params=_params(("parallel", "parallel", "arbitrary")),
        name="mla_attn",
    )(q, k, v)


BAND_TQ = 4 * CHUNK


def _band_attn_body(q_ref, k_ref, v_ref, bias_ref, o_ref, *, q_off):
    qi = pl.program_id(2) + q_off
    row_chunk = _chunk_of(lax.broadcasted_iota(jnp.int32, (BAND_TQ, KEY_TILE), 0))
    col_chunk = _chunk_of(lax.broadcasted_iota(jnp.int32, (BAND_TQ, KEY_TILE), 1))
    lane = lax.broadcasted_iota(jnp.int32, (BAND_TQ, LANES), 1)
    q_pair = q_ref[0]
    accs = []
    for h in range(2):
        qh = jnp.where(_head_lanes(lane, h), q_pair, jnp.zeros_like(q_pair))
        scores, starts = [], []
        for t in range(3):
            tile_idx = qi - 2 + t
            k0 = pl.multiple_of(jnp.maximum(tile_idx, 0) * KEY_TILE, KEY_TILE)
            s = _dot_nt(qh, k_ref[0, pl.ds(k0, KEY_TILE), :]) + bias_ref[h, t]
            if t < 2:
                s = s + jnp.where(tile_idx >= 0, 0.0, NEG)
            if t == 0:
                s = jnp.where(row_chunk <= col_chunk, s, NEG)
            elif t == 2:
                s = jnp.where(col_chunk <= row_chunk, s, NEG)
            scores.append(s)
            starts.append(k0)
        m = functools.reduce(jnp.maximum, [jnp.max(s, axis=-1, keepdims=True) for s in scores])
        acc = jnp.zeros((BAND_TQ, LANES), F32)
        for s, k0 in zip(scores, starts):
            p = jnp.exp(s - m).astype(BF16)
            acc = acc + _dot(p, _value_with_ones(v_ref[0, pl.ds(k0, KEY_TILE), :], h))
        accs.append(acc)
    o_ref[0] = _normalise_pair(accs).astype(BF16)


def _band_bias_tiles(rel_bias):
    r = jnp.arange(BAND_TQ, dtype=jnp.int32)[:, None]
    c = jnp.arange(KEY_TILE, dtype=jnp.int32)[None, :]
    tiles = []
    for t in range(3):
        rel = jnp.clip(BAND_PAST - KEY_TILE * t + r - c, -REL_CLIP, REL_CLIP) + REL_CLIP
        tiles.append(rel_bias.astype(F32)[:, rel])
    return jnp.stack(tiles, axis=1)


def _band_attn(q, k, v, bias_tiles, *, q_off):
    b, sq, _ = q.shape
    sk = k.shape[1]
    return pl.pallas_call(
        functools.partial(_band_attn_body, q_off=q_off),
        grid=(b, HEADS // 2, sq // BAND_TQ),
        in_specs=[pl.BlockSpec((1, BAND_TQ, LANES), lambda bi, p, qi: (bi, qi, p)),
                  pl.BlockSpec((1, sk, LANES), lambda bi, p, qi: (bi, 0, p)),
                  pl.BlockSpec((1, sk, LANES), lambda bi, p, qi: (bi, 0, p)),
                  pl.BlockSpec((2, 3, BAND_TQ, KEY_TILE), lambda bi, p, qi: (p, 0, 0, 0))],
        out_specs=pl.BlockSpec((1, BAND_TQ, LANES), lambda bi, p, qi: (bi, qi, p)),
        out_shape=jax.ShapeDtypeStruct((b, sq, D), BF16),
        compiler_params=_params(("parallel", "parallel", "arbitrary")),
        name="band_attn",
    )(q, k, v, bias_tiles)


def _mla_proj_body(x_ref, g_ref, wdq_ref, gq_ref, wq_ref, wkv_ref, gkv_ref,
                   qcos_ref, qsin_ref, kcos_ref, ksin_ref, q_ref, ckv_ref, kpe_ref):
    h = _rms(x_ref[...], g_ref[...]).astype(BF16)
    cq = _rms(_dot(h, wdq_ref[...]), gq_ref[...]).astype(BF16)
    qq = _dot(cq, wq_ref[...])
    qcos, qsin = qcos_ref[...], qsin_ref[...]
    half = HEADS * LANES
    for hd in range(HEADS):
        lo = LANES * hd
        q_ref[:, lo:lo + LANES] = (qq[:, lo:lo + LANES] * qcos
                                   + qq[:, half + lo:half + lo + LANES] * qsin).astype(BF16)
    kv = _dot(h, wkv_ref[...])
    ckv_ref[...] = _rms(kv[:, 0:LANES], gkv_ref[...])
    kpe_ref[...] = kv[:, LANES:2 * LANES] * kcos_ref[...] + kv[:, 2 * LANES:3 * LANES] * ksin_ref[...]


def _rope_tables(pos):
    half = MLA_ROPE // 2
    inv = ROPE_THETA ** (-jnp.arange(half, dtype=F32) / half)
    ang = pos.astype(F32)[:, None] * inv[None, :]
    cos = jnp.concatenate([jnp.cos(ang)] * 2, axis=-1)
    sin = jnp.concatenate([jnp.sin(ang)] * 2, axis=-1)
    n = pos.shape[0]
    pad = jnp.zeros((n, LANES - MLA_NOPE - MLA_ROPE), F32)
    qcos = jnp.concatenate([jnp.ones((n, MLA_NOPE), F32), cos, pad], axis=-1) * MLA_SCALE
    qsin = jnp.concatenate([jnp.zeros((n, MLA_NOPE), F32), sin, pad], axis=-1) * MLA_SCALE
    kpad = jnp.zeros((n, LANES - MLA_ROPE), F32)
    kcos = jnp.concatenate([cos, kpad], axis=-1)
    ksin = jnp.concatenate([sin, kpad], axis=-1)
    return qcos, qsin, kcos, ksin


def _rotary_partner(w):
    half = MLA_ROPE // 2
    return jnp.concatenate([-w[..., half:], w[..., :half]], axis=-1)


def _mla_weights(wuq, wdkv, wuk):
    r = wuq.shape[0]
    wq3 = wuq.reshape(r, HEADS, MLA_NOPE + MLA_ROPE)
    zpad = jnp.zeros((r, HEADS, LANES - MLA_NOPE - MLA_ROPE), wuq.dtype)
    plain = jnp.concatenate([wq3, zpad], axis=-1).reshape(r, HEADS * LANES)
    partner = jnp.concatenate([jnp.zeros((r, HEADS, MLA_NOPE), wuq.dtype),
                               _rotary_partner(wq3[..., MLA_NOPE:]), zpad], axis=-1)
    wq = jnp.concatenate([plain, partner.reshape(r, HEADS * LANES)], axis=-1).astype(BF16)
    rope_w = wdkv[:, MLA_KV_RANK:]
    kz = jnp.zeros((D, LANES - MLA_ROPE), wdkv.dtype)
    wkv = jnp.concatenate([wdkv[:, :MLA_KV_RANK], rope_w, kz, _rotary_partner(rope_w), kz],
                          axis=-1).astype(BF16)
    top = jnp.concatenate([wuk.reshape(MLA_KV_RANK, HEADS, MLA_NOPE),
                           jnp.zeros((MLA_KV_RANK, HEADS, LANES - MLA_NOPE), wuk.dtype)], axis=-1)
    place = jnp.concatenate([jnp.zeros((MLA_ROPE, MLA_NOPE), F32), jnp.eye(MLA_ROPE, dtype=F32),
                             jnp.zeros((MLA_ROPE, LANES - MLA_NOPE - MLA_ROPE), F32)], axis=-1)
    place = jnp.concatenate([place, jnp.zeros((LANES - MLA_ROPE, LANES), F32)], axis=0)
    bottom = jnp.broadcast_to(place[:, None, :], (LANES, HEADS, LANES))
    wexp = jnp.concatenate([top, bottom], axis=0).reshape(2 * LANES, HEADS * LANES).astype(BF16)
    return wq, wkv, wexp


def _mla_proj(x, g, wdq, gq, wq, wkv, gkv, tables, seq):
    rows = x.shape[0]
    tm = _row_tile(rows)
    if seq >= tm:
        n_pos_tiles = seq // tm
        pos_map = lambda i: (i % n_pos_tiles, 0)
    else:
        tables = [jnp.tile(t, (tm // seq, 1)) for t in tables]
        pos_map = lambda i: (0, 0)
    row = lambda i: (i, 0)
    return pl.pallas_call(
        _mla_proj_body,
        grid=(rows // tm,),
        in_specs=[pl.BlockSpec((tm, D), row), _resident((1, D)), _resident((D, MLA_Q_RANK)),
                  _resident((1, MLA_Q_RANK)), _resident((MLA_Q_RANK, 2 * HEADS * LANES)),
                  _resident((D, 3 * LANES)), _resident((1, MLA_KV_RANK))]
                 + [pl.BlockSpec((tm, LANES), pos_map)] * 4,
        out_specs=[pl.BlockSpec((tm, HEADS * LANES), row), pl.BlockSpec((tm, LANES), row),
                   pl.BlockSpec((tm, LANES), row)],
        out_shape=[jax.ShapeDtypeStruct((rows, HEADS * LANES), BF16),
                   jax.ShapeDtypeStruct((rows, MLA_KV_RANK), F32),
                   jax.ShapeDtypeStruct((rows, LANES), F32)],
        compiler_params=_params(("parallel",)),
        name="mla_proj",
    )(x, g.reshape(1, D), wdq, gq.reshape(1, MLA_Q_RANK), wq, wkv, gkv.reshape(1, MLA_KV_RANK), *tables)


def _mla_expand_body(ckv_ref, kpe_ref, wexp_ref, wuv_ref, k_ref, v_ref):
    c = ckv_ref[...].astype(BF16)
    cat = jnp.concatenate([c, kpe_ref[...].astype(BF16)], axis=-1)
    k_ref[...] = _dot(cat, wexp_ref[...]).astype(BF16)
    v_ref[...] = _dot(c, wuv_ref[...]).astype(BF16)


def _mla_expand(ckv, kpe, wexp, wuv):
    rows = ckv.shape[0]
    tm = _row_tile(rows)
    row = lambda i: (i, 0)
    return pl.pallas_call(
        _mla_expand_body,
        grid=(rows // tm,),
        in_specs=[pl.BlockSpec((tm, LANES), row), pl.BlockSpec((tm, LANES), row),
                  _resident((2 * LANES, HEADS * LANES)), _resident((MLA_KV_RANK, D))],
        out_specs=[pl.BlockSpec((tm, HEADS * LANES), row), pl.BlockSpec((tm, D), row)],
        out_shape=[jax.ShapeDtypeStruct((rows, HEADS * LANES), BF16),
                   jax.ShapeDtypeStruct((rows, D), BF16)],
        compiler_params=_params(("parallel",)),
        name="mla_expand",
    )(ckv, kpe, wexp, wuv)


CONV_ROWS = 16


def _conv_body(x_ref, g_ref, w1_ref, b1_ref, hist_ref, dw_ref, bdw_ref, lng_ref, lnb_ref,
               a_ref, st_ref, gbuf, ybuf, *, tm):
    @pl.when(pl.program_id(1) == 0)
    def _():
        gbuf[0:HIST_PAD, :] = hist_ref[0]

    h = _rms(x_ref[0], g_ref[...]).astype(BF16)
    u = _dot(h, w1_ref[...]) + b1_ref[...]
    gbuf[HIST_PAD:HIST_PAD + tm, :] = u[:, :D] * _sigmoid(u[:, D:])

    first = HIST_PAD - CONV_HIST

    def rows(i, _):
        r0 = pl.multiple_of(i * CONV_ROWS, CONV_ROWS)
        window = gbuf[pl.ds(r0, CONV_ROWS + HIST_PAD), :]
        acc = jnp.broadcast_to(bdw_ref[...], (CONV_ROWS, D))
        for w in range(CONV_WIDTH):
            acc = acc + dw_ref[w:w + 1, :] * window[first + w:first + w + CONV_ROWS, :]
        ybuf[pl.ds(r0, CONV_ROWS), :] = acc
        return 0

    lax.fori_loop(0, tm // CONV_ROWS, rows, 0)

    y = ybuf[...]
    yc = y - jnp.mean(y, axis=-1, keepdims=True)
    var = jnp.mean(yc * yc, axis=-1, keepdims=True)
    yn = yc * lax.rsqrt(var + EPS) * lng_ref[...] + lnb_ref[...]
    a_ref[0] = (yn * _sigmoid(yn)).astype(BF16)

    tail = gbuf[tm:tm + HIST_PAD, :]
    st_ref[0] = tail
    gbuf[0:HIST_PAD, :] = tail


def _conv(x, g, w1, b1, hist, dw, bdw, lng, lnb):
    b, s, _ = x.shape
    tm = _row_tile(s)
    hist = jnp.pad(hist.astype(F32), ((0, 0), (HIST_PAD - CONV_HIST, 0), (0, 0)))
    dw = jnp.pad(dw.astype(F32), ((0, HIST_PAD - CONV_WIDTH), (0, 0)))
    a, st = pl.pallas_call(
        functools.partial(_conv_body, tm=tm),
        grid=(b, s // tm),
        in_specs=[pl.BlockSpec((1, tm, D), lambda bi, t: (bi, t, 0)), _resident((1, D)),
                  _resident((D, 2 * D)), _resident((1, 2 * D)),
                  pl.BlockSpec((1, HIST_PAD, D), lambda bi, t: (bi, 0, 0)),
                  _resident((HIST_PAD, D)), _resident((1, D)), _resident((1, D)), _resident((1, D))],
        out_specs=[pl.BlockSpec((1, tm, D), lambda bi, t: (bi, t, 0)),
                   pl.BlockSpec((1, HIST_PAD, D), lambda bi, t: (bi, 0, 0))],
        out_shape=[jax.ShapeDtypeStruct((b, s, D), BF16),
                   jax.ShapeDtypeStruct((b, HIST_PAD, D), F32)],
        scratch_shapes=[pltpu.VMEM((HIST_PAD + tm, D), F32), pltpu.VMEM((tm, D), F32)],
        compiler_params=_params(("parallel", "arbitrary")),
        name="conv_module",
    )(x, g.reshape(1, D), w1, b1.reshape(1, 2 * D), hist, dw, bdw.reshape(1, D),
      lng.reshape(1, D), lnb.reshape(1, D))
    return a, st[:, HIST_PAD - CONV_HIST:]


def _post_body(x_ref, a_ref, p_ref, wo_ref, bo_ref, gf_ref, w13_ref, w2_ref, gp_ref, wg_ref, wp_ref,
               gfin_ref, o_ref, h_s, acc_s, *, final):
    x1 = x_ref[...] + _dot(a_ref[...], wo_ref[...]) + bo_ref[...]
    h_s[...] = _rms(x1, gf_ref[...]).astype(BF16)
    acc_s[...] = x1

    def chunk(c, _):
        uv = _dot(h_s[...], w13_ref[c])
        u = uv[:, :FFN_CHUNK]
        t = (u * _sigmoid(u) * uv[:, FFN_CHUNK:]).astype(BF16)
        acc_s[...] += _dot(t, w2_ref[c])
        return 0

    lax.fori_loop(0, N_FFN_CHUNKS, chunk, 0)
    x2 = acc_s[...]
    gate = _sigmoid(_dot(_rms(x2, gp_ref[...]).astype(BF16), wg_ref[...]))
    x3 = x2 + gate * _dot(p_ref[...].astype(BF16), wp_ref[...])
    o_ref[...] = _rms(x3, gfin_ref[...]) if final else x3


def _post(x, a, p, wo, bo, gf, w13, w2, gp, wg, wp, gfin, *, final):
    rows = x.shape[0]
    tm = _row_tile(rows)
    row = lambda i: (i, 0)
    return pl.pallas_call(
        functools.partial(_post_body, final=final),
        grid=(rows // tm,),
        in_specs=[pl.BlockSpec((tm, D), row), pl.BlockSpec((tm, D), row), pl.BlockSpec((tm, PLE_DIM), row),
                  _resident((D, D)), _resident((1, D)), _resident((1, D)),
                  _resident((N_FFN_CHUNKS, D, 2 * FFN_CHUNK)), _resident((N_FFN_CHUNKS, FFN_CHUNK, D)),
                  _resident((1, D)), _resident((D, D)), _resident((PLE_DIM, D)), _resident((1, D))],
        out_specs=pl.BlockSpec((tm, D), row),
        out_shape=jax.ShapeDtypeStruct((rows, D), F32),
        scratch_shapes=[pltpu.VMEM((tm, D), BF16), pltpu.VMEM((tm, D), F32)],
        compiler_params=_params(("parallel",)),
        name="post_mixer",
    )(x, a, p, wo, bo.reshape(1, D), gf.reshape(1, D), w13, w2, gp.reshape(1, D), wg, wp,
      gfin.reshape(1, D))


def _ffn_weights(w1, w3, w2):
    split = lambda w: w.reshape(D, N_FFN_CHUNKS, FFN_CHUNK).transpose(1, 0, 2)
    w13 = jnp.concatenate([split(w1), split(w3)], axis=-1).astype(BF16)
    return w13, w2.reshape(N_FFN_CHUNKS, FFN_CHUNK, D).astype(BF16)


def _with_cache(cache, new, total):
    b, past, w = cache.shape
    pad = jnp.zeros((b, total - past - new.shape[1], w), new.dtype)
    return jnp.concatenate([cache.astype(new.dtype), new, pad], axis=1)


def kernel(x_prompt, x_sample, p_prompt, p_sample, cache_sb_k, cache_sb_v, cache_mla_ckv, cache_mla_kpe, state_conv, cache_band_k, cache_band_v, norm_mix, norm_ffn, norm_ple, norm_final, ffn_w1, ffn_w3, ffn_w2, ple_proj, ple_gate, sb_wqkv, sb_wo, mla_wdq, mla_gq, mla_wuq, mla_wdkv, mla_gkv, mla_wuk, mla_wuv, mla_wo, conv_w1, conv_b1, conv_dw, conv_bdw, conv_ln_g, conv_ln_b, conv_w2, conv_b2, band_wqkv, band_rel_bias, band_wo):
    bp, s, _ = x_prompt.shape
    bs, t, _ = x_sample.shape
    depth = norm_mix.shape[0]
    past = cache_sb_k.shape[2]
    xp = x_prompt.reshape(bp * s, D)
    xs = x_sample.reshape(bs * t, D)
    pos_p = jnp.arange(s, dtype=jnp.int32)
    pos_s = past + jnp.arange(t, dtype=jnp.int32)
    zero_bias = jnp.zeros((D,), F32)
    head_scale = HEAD_DIM ** -0.5
    sample_keys = -(-(past + t) // KEY_TILE) * KEY_TILE

    outs = {name: [] for name in ("sb_kp", "sb_vp", "sb_ks", "sb_vs", "mla_cp", "mla_pp", "mla_cs", "mla_ps",
                                  "conv_p", "conv_s", "band_kp", "band_vp", "band_ks", "band_vs")}

    for i in range(depth):
        m, j = i % N_MIXERS, i // N_MIXERS
        if m == 0:
            w = sb_wqkv[j].astype(BF16)
            q, kf, vf, kb, vb = _qkv(xp, norm_mix[i], w, head_scale)
            ap = _sb_attn(q.reshape(bp, s, D), kb.reshape(bp, s, D), vb.reshape(bp, s, D),
                          tq=ROW_TILE, q_off=0)
            outs["sb_kp"].append(kf.reshape(bp, s, HEADS, HEAD_DIM))
            outs["sb_vp"].append(vf.reshape(bp, s, HEADS, HEAD_DIM))
            q, kf, vf, kb, vb = _qkv(xs, norm_mix[i], w, head_scale)
            k_all = _with_cache(cache_sb_k[j].reshape(bs, past, D), kb.reshape(bs, t, D), sample_keys)
            v_all = _with_cache(cache_sb_v[j].reshape(bs, past, D), vb.reshape(bs, t, D), sample_keys)
            as_ = _sb_attn(q.reshape(bs, t, D), k_all, v_all, tq=t, q_off=past // t)
            outs["sb_ks"].append(kf.reshape(bs, t, HEADS, HEAD_DIM))
            outs["sb_vs"].append(vf.reshape(bs, t, HEADS, HEAD_DIM))
            wo, bo = sb_wo[j].astype(BF16), zero_bias
        elif m == 1:
            wq, wkv, wexp = _mla_weights(mla_wuq[j], mla_wdkv[j], mla_wuk[j])
            wdq, wuv = mla_wdq[j].astype(BF16), mla_wuv[j].astype(BF16)
            q, ckv, kpe = _mla_proj(xp, norm_mix[i], wdq, mla_gq[j], wq, wkv, mla_gkv[j],
                                    _rope_tables(pos_p), s)
            k, v = _mla_expand(ckv, kpe, wexp, wuv)
            ap = _softmax_attn(q.reshape(bp, s, -1), k.reshape(bp, s, -1), v.reshape(bp, s, D),
                               tq=ROW_TILE, q_off=0)
            outs["mla_cp"].append(ckv.reshape(bp, s, MLA_KV_RANK))
            outs["mla_pp"].append(kpe[:, :MLA_ROPE].reshape(bp, s, MLA_ROPE))
            q, ckv, kpe = _mla_proj(xs, norm_mix[i], wdq, mla_gq[j], wq, wkv, mla_gkv[j],
                                    _rope_tables(pos_s), t)
            ckv_all = _with_cache(cache_mla_ckv[j], ckv.reshape(bs, t, MLA_KV_RANK), sample_keys)
            kpe_cache = jnp.pad(cache_mla_kpe[j], ((0, 0), (0, 0), (0, LANES - MLA_ROPE)))
            kpe_all = _with_cache(kpe_cache, kpe.reshape(bs, t, LANES), sample_keys)
            k, v = _mla_expand(ckv_all.reshape(bs * sample_keys, MLA_KV_RANK),
                               kpe_all.reshape(bs * sample_keys, LANES), wexp, wuv)
            as_ = _softmax_attn(q.reshape(bs, t, -1), k.reshape(bs, sample_keys, -1),
                                v.reshape(bs, sample_keys, D), tq=t, q_off=past // t)
            outs["mla_cs"].append(ckv.reshape(bs, t, MLA_KV_RANK))
            outs["mla_ps"].append(kpe[:, :MLA_ROPE].reshape(bs, t, MLA_ROPE))
            wo, bo = mla_wo[j].astype(BF16), zero_bias
        elif m == 2:
            w1 = conv_w1[j].astype(BF16)
            ap, st = _conv(xp.reshape(bp, s, D), norm_mix[i], w1, conv_b1[j],
                           jnp.zeros((bp, CONV_HIST, D), F32), conv_dw[j], conv_bdw[j],
                           conv_ln_g[j], conv_ln_b[j])
            outs["conv_p"].append(st)
            as_, st = _conv(xs.reshape(bs, t, D), norm_mix[i], w1, conv_b1[j], state_conv[j],
                            conv_dw[j], conv_bdw[j], conv_ln_g[j], conv_ln_b[j])
            outs["conv_s"].append(st)
            wo, bo = conv_w2[j].astype(BF16), conv_b2[j]
        else:
            w = band_wqkv[j].astype(BF16)
            bias_tiles = _band_bias_tiles(band_rel_bias[j])
            q, kf, vf, kb, vb = _qkv(xp, norm_mix[i], w, head_scale)
            ap = _band_attn(q.reshape(bp, s, D), kb.reshape(bp, s, D), vb.reshape(bp, s, D),
                            bias_tiles, q_off=0)
            keep = min(BAND_PAST, s)
            outs["band_kp"].append(kf.reshape(bp, s, HEADS, HEAD_DIM)[:, s - keep:])
            outs["band_vp"].append(vf.reshape(bp, s, HEADS, HEAD_DIM)[:, s - keep:])
            q, kf, vf, kb, vb = _qkv(xs, norm_mix[i], w, head_scale)
            n_past = cache_band_k.shape[2]
            local = n_past + BAND_TQ
            k_all = _with_cache(cache_band_k[j].reshape(bs, n_past, D), kb.reshape(bs, t, D), local)
            v_all = _with_cache(cache_band_v[j].reshape(bs, n_past, D), vb.reshape(bs, t, D), local)
            q_pad = jnp.pad(q.reshape(bs, t, D), ((0, 0), (0, BAND_TQ - t), (0, 0)))
            as_ = _band_attn(q_pad, k_all, v_all, bias_tiles, q_off=n_past // BAND_TQ)[:, :t]
            kk = jnp.concatenate([cache_band_k[j], kf.reshape(bs, t, HEADS, HEAD_DIM)], axis=1)
            vv = jnp.concatenate([cache_band_v[j], vf.reshape(bs, t, HEADS, HEAD_DIM)], axis=1)
            outs["band_ks"].append(kk[:, t:])
            outs["band_vs"].append(vv[:, t:])
            wo, bo = band_wo[j].astype(BF16), zero_bias

        w13, w2 = _ffn_weights(ffn_w1[i], ffn_w3[i], ffn_w2[i])
        final = i == depth - 1
        post = functools.partial(_post, wo=wo, bo=bo, gf=norm_ffn[i], w13=w13, w2=w2, gp=norm_ple[i],
                                 wg=ple_gate[i].astype(BF16), wp=ple_proj[i].astype(BF16),
                                 gfin=norm_final, final=final)
        xp = post(xp, ap.reshape(bp * s, D), p_prompt[i].reshape(bp * s, PLE_DIM))
        xs = post(xs, as_.reshape(bs * t, D), p_sample[i].reshape(bs * t, PLE_DIM))

    stack = lambda name: jnp.stack(outs[name])
    return (xp.reshape(bp, s, D), xs.reshape(bs, t, D),
            stack("sb_kp"), stack("sb_vp"), stack("sb_ks"), stack("sb_vs"),
            stack("mla_cp"), stack("mla_pp"), stack("mla_cs"), stack("mla_ps"),
            stack("conv_p"), stack("conv_s"),
            stack("band_kp"), stack("band_vp"), stack("band_ks"), stack("band_vs"))
```

```python
import functools
import math

import jax
import jax.numpy as jnp
from jax import lax
from jax.experimental import pallas as pl
from jax.experimental.pallas import tpu as pltpu

F32 = jnp.float32
BF16 = jnp.bfloat16

D = 1024
EPS = 1e-6
NEG = -1e30
HEADS = 16
HEAD_DIM = 64
CHUNK = 64
PLE_DIM = 256
FFN_HIDDEN = 2816
FFN_CHUNK = 256
N_FFN_CHUNKS = FFN_HIDDEN // FFN_CHUNK
MLA_Q_RANK = 256
MLA_KV_RANK = 128
MLA_NOPE = 64
MLA_ROPE = 32
MLA_SCALE = (MLA_NOPE + MLA_ROPE) ** -0.5
ROPE_THETA = 10000.0
CONV_WIDTH = 31
CONV_HIST = CONV_WIDTH - 1
HIST_PAD = 32
BAND_CHUNKS = 8
BAND_PAST = BAND_CHUNKS * CHUNK
REL_CLIP = 128
N_MIXERS = 4

LANES = 128
KEY_TILE = 256
SM_KEY_TILE = 2 * KEY_TILE
LOG2E = math.log2(math.e)
ROW_TILE = 512
VMEM_LIMIT = 56 * 1024 * 1024


def _dot(a, b):
    return jnp.dot(a, b, preferred_element_type=F32)


def _dot_nt(a, b):
    return lax.dot_general(a, b, (((1,), (1,)), ((), ())), preferred_element_type=F32)


def _rms(x, g):
    return x * lax.rsqrt(jnp.mean(x * x, axis=-1, keepdims=True) + EPS) * g


def _sigmoid(x):
    return 1.0 / (1.0 + jnp.exp(-x))


def _params(sem):
    return pltpu.CompilerParams(dimension_semantics=sem, vmem_limit_bytes=VMEM_LIMIT)


def _resident(shape):
    zeros = (0,) * len(shape)
    return pl.BlockSpec(shape, lambda *_: zeros, pipeline_mode=pl.Buffered(1))


def _row_tile(rows):
    return min(ROW_TILE, rows)


def _chunk_of(idx):
    return lax.shift_right_arithmetic(idx, CHUNK.bit_length() - 1)


def _head_lanes(lane, h):
    return lane < HEAD_DIM if h == 0 else lane >= HEAD_DIM


def _qkv_body(x_ref, g_ref, w_ref, q_ref, kf_ref, vf_ref, kb_ref, vb_ref, *, scale):
    h = _rms(x_ref[...], g_ref[...]).astype(BF16)
    q_ref[...] = (_dot(h, w_ref[:, 0:D]) * scale).astype(BF16)
    k = _dot(h, w_ref[:, D:2 * D])
    kf_ref[...] = k
    kb_ref[...] = k.astype(BF16)
    v = _dot(h, w_ref[:, 2 * D:3 * D])
    vf_ref[...] = v
    vb_ref[...] = v.astype(BF16)


def _qkv(x, g, w, scale):
    rows = x.shape[0]
    tm = _row_tile(rows)
    row = lambda i: (i, 0)
    out_sds = lambda dt: jax.ShapeDtypeStruct((rows, D), dt)
    return pl.pallas_call(
        functools.partial(_qkv_body, scale=scale),
        grid=(rows // tm,),
        in_specs=[pl.BlockSpec((tm, D), row), _resident((1, D)), _resident((D, 3 * D))],
        out_specs=[pl.BlockSpec((tm, D), row)] * 5,
        out_shape=[out_sds(BF16), out_sds(F32), out_sds(F32), out_sds(BF16), out_sds(BF16)],
        compiler_params=_params(("parallel",)),
        name="qkv_proj",
    )(x, g.reshape(1, D), w)


def _sb_attn_body(q_ref, k_ref, v_ref, tri_ref, o_ref, *, tq, q_off):
    q0 = (pl.program_id(2) + q_off) * tq
    first_diag = lax.div(q0, KEY_TILE)
    n_diag = max(1, tq // KEY_TILE)
    lane = lax.broadcasted_iota(jnp.int32, (tq, LANES), 1)
    key_lane = lax.broadcasted_iota(jnp.int32, (KEY_TILE, LANES), 1)
    diff = (lax.broadcasted_iota(jnp.int32, (tq, KEY_TILE), 1)
            - lax.broadcasted_iota(jnp.int32, (tq, KEY_TILE), 0))
    col_minus_row = jnp.concatenate([diff, diff], axis=1)
    q_pair = q_ref[0]
    tri = tri_ref[...]

    def by_head(tile_rows):
        zero = jnp.zeros_like(tile_rows)
        return jnp.concatenate([jnp.where(_head_lanes(key_lane, h), tile_rows, zero) for h in range(2)], axis=0)

    def tile(j, masked):
        k0 = pl.multiple_of(j * KEY_TILE, KEY_TILE)
        z = _dot_nt(q_pair, by_head(k_ref[0, pl.ds(k0, KEY_TILE), :]))
        neg_abs = pltpu.bitcast(pltpu.bitcast(z, jnp.uint32) | jnp.uint32(0x80000000), F32)
        drop = jnp.maximum(z, 0.0) + jnp.log(1.0 + jnp.exp(neg_abs))
        if masked:
            valid = col_minus_row < (q0 - k0)
            drop = jnp.where(valid, drop, 0.0)
        hi = drop.astype(BF16)
        later = jnp.concatenate([_dot(hi[:, :KEY_TILE], tri), _dot(hi[:, KEY_TILE:], tri)], axis=1)
        w = jnp.exp((z - drop) - later)
        if masked:
            w = jnp.where(valid, w, 0.0)
        o = _dot(w.astype(BF16), by_head(v_ref[0, pl.ds(k0, KEY_TILE), :]))
        total = jnp.where(lane < HEAD_DIM, jnp.sum(drop[:, :KEY_TILE], axis=-1, keepdims=True),
                          jnp.sum(drop[:, KEY_TILE:], axis=-1, keepdims=True))
        return o, total

    def step(tiles, carry, masked):
        dropped, out = carry
        for j in tiles:
            o, total = tile(j, masked)
            out = out + jnp.exp(-dropped) * o
            dropped = dropped + total
        return dropped, out

    carry = (jnp.zeros((tq, LANES), F32), jnp.zeros((tq, LANES), F32))
    carry = step([first_diag + d for d in reversed(range(n_diag))], carry, True)
    carry = lax.fori_loop(
        0, lax.div(first_diag, 2),
        lambda t, c: step([first_diag - 1 - 2 * t, first_diag - 2 - 2 * t], c, False), carry)
    o_ref[0] = carry[1].astype(BF16)


def _tri():
    j = lax.broadcasted_iota(jnp.int32, (KEY_TILE, KEY_TILE), 0)
    s = lax.broadcasted_iota(jnp.int32, (KEY_TILE, KEY_TILE), 1)
    return (j > s).astype(BF16)


def _sb_attn(q, k, v, *, tq, q_off):
    b, sq, _ = q.shape
    sk = k.shape[1]
    assert (q_off * tq) % (2 * KEY_TILE) == 0 and (tq % (2 * KEY_TILE) == 0 or sq == tq)
    assert (q_off + sq // tq - 1) * tq + max(tq, KEY_TILE) <= sk
    return pl.pallas_call(
        functools.partial(_sb_attn_body, tq=tq, q_off=q_off),
        grid=(b, HEADS // 2, sq // tq),
        in_specs=[pl.BlockSpec((1, tq, LANES), lambda bi, p, qi: (bi, qi, p)),
                  pl.BlockSpec((1, sk, LANES), lambda bi, p, qi: (bi, 0, p)),
                  pl.BlockSpec((1, sk, LANES), lambda bi, p, qi: (bi, 0, p)),
                  pl.BlockSpec((KEY_TILE, KEY_TILE), lambda bi, p, qi: (0, 0))],
        out_specs=pl.BlockSpec((1, tq, LANES), lambda bi, p, qi: (bi, qi, p)),
        out_shape=jax.ShapeDtypeStruct((b, sq, D), BF16),
        compiler_params=_params(("parallel", "parallel", "arbitrary")),
        name="sb_attn",
    )(q, k, v, _tri())


def _value_with_ones(v_tile, h):
    lane = lax.broadcasted_iota(jnp.int32, v_tile.shape, 1)
    return jnp.where(_head_lanes(lane, h), v_tile, jnp.ones_like(v_tile))


def _normalise_pair(accs):
    res = [a * pltpu.roll(1.0 / a, HEAD_DIM, 1) for a in accs]
    lane = lax.broadcasted_iota(jnp.int32, res[0].shape, 1)
    return jnp.where(lane < HEAD_DIM, res[0], res[1])


def _softmax_attn_body(q_ref, k_ref, v_ref, o_ref, *, tq, q_off):
    q0 = (pl.program_id(2) + q_off) * tq
    last = lax.div(q0, SM_KEY_TILE)
    chunk_diff = (_chunk_of(lax.broadcasted_iota(jnp.int32, (tq, SM_KEY_TILE), 1))
                  - _chunk_of(lax.broadcasted_iota(jnp.int32, (tq, SM_KEY_TILE), 0)))

    def step(tiles, carry, masked):
        new = list(carry)
        for j in tiles:
            k0 = pl.multiple_of(j * SM_KEY_TILE, SM_KEY_TILE)
            v_tile = v_ref[0, pl.ds(k0, SM_KEY_TILE), :]
            for h in range(2):
                m, acc = new[h]
                s = _dot_nt(q_ref[0, :, LANES * h:LANES * (h + 1)],
                            k_ref[0, pl.ds(k0, SM_KEY_TILE), LANES * h:LANES * (h + 1)])
                if masked:
                    s = jnp.where(chunk_diff <= _chunk_of(q0 - k0), s, NEG)
                m_new = jnp.maximum(m, jnp.max(s, axis=-1, keepdims=True))
                p = jnp.exp2(s - m_new)
                acc = acc * jnp.exp2(m - m_new) + _dot(p.astype(BF16), _value_with_ones(v_tile, h))
                new[h] = (m_new, acc)
        return tuple(new)

    carry = tuple((jnp.full((tq, 1), NEG, F32), jnp.zeros((tq, LANES), F32)) for _ in range(2))
    pairs = lax.div(last, 2)
    carry = lax.fori_loop(0, pairs, lambda t, c: step([2 * t, 2 * t + 1], c, False), carry)
    carry = lax.fori_loop(2 * pairs, last, lambda t, c: step([t], c, False), carry)
    carry = step([last], carry, True)
    o_ref[0] = _normalise_pair([carry[0][1], carry[1][1]]).astype(BF16)


def _softmax_attn(q, k, v, *, tq, q_off):
    b, sq, _ = q.shape
    sk = k.shape[1]
    assert tq <= SM_KEY_TILE and SM_KEY_TILE % tq == 0 and sk % SM_KEY_TILE == 0
    return pl.pallas_call(
        functools.partial(_softmax_attn_body, tq=tq, q_off=q_off),
        grid=(b, HEADS // 2, sq // tq),
        in_specs=[pl.BlockSpec((1, tq, 2 * LANES), lambda bi, p, qi: (bi, qi, p)),
                  pl.BlockSpec((1, sk, 2 * LANES), lambda bi, p, qi: (bi, 0, p)),
                  pl.BlockSpec((1, sk, LANES), lambda bi, p, qi: (bi, 0, p))],
        out_specs=pl.BlockSpec((1, tq, LANES), lambda bi, p, qi: (bi, qi, p)),
        out_shape=jax.ShapeDtypeStruct((b, sq, D), BF16),
        compiler_params=_params(("parallel", "parallel", "arbitrary")),
        name="mla_attn",
    )(q, k, v)


BAND_TQ = 4 * CHUNK


def _band_attn_body(q_ref, k_ref, v_ref, bias_ref, o_ref, *, q_off):
    qi = pl.program_id(2) + q_off
    row_chunk = _chunk_of(lax.broadcasted_iota(jnp.int32, (BAND_TQ, KEY_TILE), 0))
    col_chunk = _chunk_of(lax.broadcasted_iota(jnp.int32, (BAND_TQ, KEY_TILE), 1))
    lane = lax.broadcasted_iota(jnp.int32, (BAND_TQ, LANES), 1)
    q_pair = q_ref[0]
    accs = []
    for h in range(2):
        qh = jnp.where(_head_lanes(lane, h), q_pair, jnp.zeros_like(q_pair))
        scores, starts = [], []
        for t in range(3):
            tile_idx = qi - 2 + t
            k0 = pl.multiple_of(jnp.maximum(tile_idx, 0) * KEY_TILE, KEY_TILE)
            s = _dot_nt(qh, k_ref[0, pl.ds(k0, KEY_TILE), :]) + bias_ref[h, t]
            if t < 2:
                s = s + jnp.where(tile_idx >= 0, 0.0, NEG)
            if t == 0:
                s = jnp.where(row_chunk <= col_chunk, s, NEG)
            elif t == 2:
                s = jnp.where(col_chunk <= row_chunk, s, NEG)
            scores.append(s)
            starts.append(k0)
        m = functools.reduce(jnp.maximum, [jnp.max(s, axis=-1, keepdims=True) for s in scores])
        acc = jnp.zeros((BAND_TQ, LANES), F32)
        for s, k0 in zip(scores, starts):
            p = jnp.exp2(s - m).astype(BF16)
            acc = acc + _dot(p, _value_with_ones(v_ref[0, pl.ds(k0, KEY_TILE), :], h))
        accs.append(acc)
    o_ref[0] = _normalise_pair(accs).astype(BF16)


def _band_bias_tiles(rel_bias):
    table = rel_bias.astype(F32) * LOG2E
    heads = table.shape[0]
    n = BAND_TQ + KEY_TILE - 1
    tiles = []
    for t in range(3):
        d_lo = BAND_PAST - KEY_TILE * t - (KEY_TILE - 1)
        below = max(0, min(n, -REL_CLIP - d_lo))
        above = max(0, min(n, d_lo + n - 1 - REL_CLIP))
        mid_lo = d_lo + below + REL_CLIP
        diag = jnp.concatenate([jnp.broadcast_to(table[:, :1], (heads, below)),
                                table[:, mid_lo:mid_lo + n - below - above],
                                jnp.broadcast_to(table[:, -1:], (heads, above))], axis=1)
        rows = jnp.tile(diag, (1, BAND_TQ + 1))[:, :BAND_TQ * (n + 1)].reshape(heads, BAND_TQ, n + 1)
        tiles.append(rows[:, :, KEY_TILE - 1::-1])
    return jnp.stack(tiles, axis=1)


def _band_attn(q, k, v, bias_tiles, *, q_off):
    b, sq, _ = q.shape
    sk = k.shape[1]
    return pl.pallas_call(
        functools.partial(_band_attn_body, q_off=q_off),
        grid=(b, HEADS // 2, sq // BAND_TQ),
        in_specs=[pl.BlockSpec((1, BAND_TQ, LANES), lambda bi, p, qi: (bi, qi, p)),
                  pl.BlockSpec((1, sk, LANES), lambda bi, p, qi: (bi, 0, p)),
                  pl.BlockSpec((1, sk, LANES), lambda bi, p, qi: (bi, 0, p)),
                  pl.BlockSpec((2, 3, BAND_TQ, KEY_TILE), lambda bi, p, qi: (p, 0, 0, 0))],
        out_specs=pl.BlockSpec((1, BAND_TQ, LANES), lambda bi, p, qi: (bi, qi, p)),
        out_shape=jax.ShapeDtypeStruct((b, sq, D), BF16),
        compiler_params=_params(("parallel", "parallel", "arbitrary")),
        name="band_attn",
    )(q, k, v, bias_tiles)


def _mla_proj_body(x_ref, g_ref, wdq_ref, gq_ref, wq_ref, wkv_ref, gkv_ref,
                   qcos_ref, qsin_ref, kcos_ref, ksin_ref, q_ref, ckv_ref, kpe_ref):
    h = _rms(x_ref[...], g_ref[...]).astype(BF16)
    cq = _rms(_dot(h, wdq_ref[...]), gq_ref[...]).astype(BF16)
    qq = _dot(cq, wq_ref[...])
    qcos, qsin = qcos_ref[...], qsin_ref[...]
    half = HEADS * LANES
    for hd in range(HEADS):
        lo = LANES * hd
        q_ref[:, lo:lo + LANES] = (qq[:, lo:lo + LANES] * qcos
                                   + qq[:, half + lo:half + lo + LANES] * qsin).astype(BF16)
    kv = _dot(h, wkv_ref[...])
    ckv_ref[...] = _rms(kv[:, 0:LANES], gkv_ref[...])
    kpe_ref[...] = kv[:, LANES:2 * LANES] * kcos_ref[...] + kv[:, 2 * LANES:3 * LANES] * ksin_ref[...]


def _rope_tables(pos):
    half = MLA_ROPE // 2
    inv = ROPE_THETA ** (-jnp.arange(half, dtype=F32) / half)
    ang = pos.astype(F32)[:, None] * inv[None, :]
    cos = jnp.concatenate([jnp.cos(ang)] * 2, axis=-1)
    sin = jnp.concatenate([jnp.sin(ang)] * 2, axis=-1)
    n = pos.shape[0]
    pad = jnp.zeros((n, LANES - MLA_NOPE - MLA_ROPE), F32)
    q_scale = MLA_SCALE * LOG2E
    qcos = jnp.concatenate([jnp.ones((n, MLA_NOPE), F32), cos, pad], axis=-1) * q_scale
    qsin = jnp.concatenate([jnp.zeros((n, MLA_NOPE), F32), sin, pad], axis=-1) * q_scale
    kpad = jnp.zeros((n, LANES - MLA_ROPE), F32)
    kcos = jnp.concatenate([cos, kpad], axis=-1)
    ksin = jnp.concatenate([sin, kpad], axis=-1)
    return qcos, qsin, kcos, ksin


def _rotary_partner(w):
    half = MLA_ROPE // 2
    return jnp.concatenate([-w[..., half:], w[..., :half]], axis=-1)


def _mla_weights(wuq, wdkv, wuk):
    r = wuq.shape[0]
    wq3 = wuq.reshape(r, HEADS, MLA_NOPE + MLA_ROPE)
    zpad = jnp.zeros((r, HEADS, LANES - MLA_NOPE - MLA_ROPE), wuq.dtype)
    plain = jnp.concatenate([wq3, zpad], axis=-1).reshape(r, HEADS * LANES)
    partner = jnp.concatenate([jnp.zeros((r, HEADS, MLA_NOPE), wuq.dtype),
                               _rotary_partner(wq3[..., MLA_NOPE:]), zpad], axis=-1)
    wq = jnp.concatenate([plain, partner.reshape(r, HEADS * LANES)], axis=-1).astype(BF16)
    rope_w = wdkv[:, MLA_KV_RANK:]
    kz = jnp.zeros((D, LANES - MLA_ROPE), wdkv.dtype)
    wkv = jnp.concatenate([wdkv[:, :MLA_KV_RANK], rope_w, kz, _rotary_partner(rope_w), kz],
                          axis=-1).astype(BF16)
    top = jnp.concatenate([wuk.reshape(MLA_KV_RANK, HEADS, MLA_NOPE),
                           jnp.zeros((MLA_KV_RANK, HEADS, LANES - MLA_NOPE), wuk.dtype)], axis=-1)
    place = jnp.concatenate([jnp.zeros((MLA_ROPE, MLA_NOPE), F32), jnp.eye(MLA_ROPE, dtype=F32),
                             jnp.zeros((MLA_ROPE, LANES - MLA_NOPE - MLA_ROPE), F32)], axis=-1)
    place = jnp.concatenate([place, jnp.zeros((LANES - MLA_ROPE, LANES), F32)], axis=0)
    bottom = jnp.broadcast_to(place[:, None, :], (LANES, HEADS, LANES))
    wexp = jnp.concatenate([top, bottom], axis=0).reshape(2 * LANES, HEADS * LANES).astype(BF16)
    return wq, wkv, wexp


def _mla_proj(x, g, wdq, gq, wq, wkv, gkv, tables, seq):
    rows = x.shape[0]
    tm = _row_tile(rows)
    if seq >= tm:
        n_pos_tiles = seq // tm
        pos_map = lambda i: (i % n_pos_tiles, 0)
    else:
        tables = [jnp.tile(t, (tm // seq, 1)) for t in tables]
        pos_map = lambda i: (0, 0)
    row = lambda i: (i, 0)
    return pl.pallas_call(
        _mla_proj_body,
        grid=(rows // tm,),
        in_specs=[pl.BlockSpec((tm, D), row), _resident((1, D)), _resident((D, MLA_Q_RANK)),
                  _resident((1, MLA_Q_RANK)), _resident((MLA_Q_RANK, 2 * HEADS * LANES)),
                  _resident((D, 3 * LANES)), _resident((1, MLA_KV_RANK))]
                 + [pl.BlockSpec((tm, LANES), pos_map)] * 4,
        out_specs=[pl.BlockSpec((tm, HEADS * LANES), row), pl.BlockSpec((tm, LANES), row),
                   pl.BlockSpec((tm, LANES), row)],
        out_shape=[jax.ShapeDtypeStruct((rows, HEADS * LANES), BF16),
                   jax.ShapeDtypeStruct((rows, MLA_KV_RANK), F32),
                   jax.ShapeDtypeStruct((rows, LANES), F32)],
        compiler_params=_params(("parallel",)),
        name="mla_proj",
    )(x, g.reshape(1, D), wdq, gq.reshape(1, MLA_Q_RANK), wq, wkv, gkv.reshape(1, MLA_KV_RANK), *tables)


def _mla_expand_body(ckv_ref, kpe_ref, wexp_ref, wuv_ref, k_ref, v_ref):
    c = ckv_ref[...].astype(BF16)
    cat = jnp.concatenate([c, kpe_ref[...].astype(BF16)], axis=-1)
    k_ref[...] = _dot(cat, wexp_ref[...]).astype(BF16)
    v_ref[...] = _dot(c, wuv_ref[...]).astype(BF16)


def _mla_expand(ckv, kpe, wexp, wuv):
    rows = ckv.shape[0]
    tm = _row_tile(rows)
    row = lambda i: (i, 0)
    return pl.pallas_call(
        _mla_expand_body,
        grid=(rows // tm,),
        in_specs=[pl.BlockSpec((tm, LANES), row), pl.BlockSpec((tm, LANES), row),
                  _resident((2 * LANES, HEADS * LANES)), _resident((MLA_KV_RANK, D))],
        out_specs=[pl.BlockSpec((tm, HEADS * LANES), row), pl.BlockSpec((tm, D), row)],
        out_shape=[jax.ShapeDtypeStruct((rows, HEADS * LANES), BF16),
                   jax.ShapeDtypeStruct((rows, D), BF16)],
        compiler_params=_params(("parallel",)),
        name="mla_expand",
    )(ckv, kpe, wexp, wuv)


CONV_ROWS = 32
LANE_COLS = D // LANES


def _conv_body(x_ref, g_ref, w1_ref, b1_ref, hist_ref, dw_ref, bdw_ref, lng_ref, lnb_ref,
               a_ref, st_ref, gbuf, ybuf, *, tm):
    @pl.when(pl.program_id(1) == 0)
    def _():
        for c in range(LANE_COLS):
            gbuf[c, 0:HIST_PAD, :] = hist_ref[0, :, LANES * c:LANES * (c + 1)]

    h = _rms(x_ref[0], g_ref[...]).astype(BF16)
    u = _dot(h, w1_ref[...]) + b1_ref[...]
    glu = u[:, :D] * _sigmoid(u[:, D:])
    for c in range(LANE_COLS):
        gbuf[c, HIST_PAD:HIST_PAD + tm, :] = glu[:, LANES * c:LANES * (c + 1)]

    first = HIST_PAD - CONV_HIST

    def rows(i, _):
        r0 = pl.multiple_of(i * CONV_ROWS, CONV_ROWS)
        for c in range(LANE_COLS):
            cols = slice(LANES * c, LANES * (c + 1))
            acc = jnp.broadcast_to(bdw_ref[:, cols], (CONV_ROWS, LANES))
            for w in range(CONV_WIDTH):
                acc = acc + dw_ref[w:w + 1, cols] * gbuf[c, pl.ds(r0 + first + w, CONV_ROWS), :]
            ybuf[pl.ds(r0, CONV_ROWS), cols] = acc
        return 0

    lax.fori_loop(0, tm // CONV_ROWS, rows, 0)

    y = ybuf[...]
    yc = y - jnp.mean(y, axis=-1, keepdims=True)
    var = jnp.mean(yc * yc, axis=-1, keepdims=True)
    yn = yc * lax.rsqrt(var + EPS) * lng_ref[...] + lnb_ref[...]
    a_ref[0] = (yn * _sigmoid(yn)).astype(BF16)

    for c in range(LANE_COLS):
        tail = gbuf[c, tm:tm + HIST_PAD, :]
        st_ref[0, :, LANES * c:LANES * (c + 1)] = tail
        gbuf[c, 0:HIST_PAD, :] = tail


def _conv(x, g, w1, b1, hist, dw, bdw, lng, lnb):
    b, s, _ = x.shape
    tm = _row_tile(s)
    hist = jnp.pad(hist.astype(F32), ((0, 0), (HIST_PAD - CONV_HIST, 0), (0, 0)))
    dw = jnp.pad(dw.astype(F32), ((0, HIST_PAD - CONV_WIDTH), (0, 0)))
    a, st = pl.pallas_call(
        functools.partial(_conv_body, tm=tm),
        grid=(b, s // tm),
        in_specs=[pl.BlockSpec((1, tm, D), lambda bi, t: (bi, t, 0)), _resident((1, D)),
                  _resident((D, 2 * D)), _resident((1, 2 * D)),
                  pl.BlockSpec((1, HIST_PAD, D), lambda bi, t: (bi, 0, 0)),
                  _resident((HIST_PAD, D)), _resident((1, D)), _resident((1, D)), _resident((1, D))],
        out_specs=[pl.BlockSpec((1, tm, D), lambda bi, t: (bi, t, 0)),
                   pl.BlockSpec((1, HIST_PAD, D), lambda bi, t: (bi, 0, 0))],
        out_shape=[jax.ShapeDtypeStruct((b, s, D), BF16),
                   jax.ShapeDtypeStruct((b, HIST_PAD, D), F32)],
        scratch_shapes=[pltpu.VMEM((LANE_COLS, HIST_PAD + tm, LANES), F32), pltpu.VMEM((tm, D), F32)],
        compiler_params=_params(("parallel", "arbitrary")),
        name="conv_module",
    )(x, g.reshape(1, D), w1, b1.reshape(1, 2 * D), hist, dw, bdw.reshape(1, D),
      lng.reshape(1, D), lnb.reshape(1, D))
    return a, st[:, HIST_PAD - CONV_HIST:]


def _post_body(x_ref, a_ref, p_ref, wo_ref, bo_ref, gf_ref, w13_ref, w2_ref, gp_ref, wg_ref, wp_ref,
               gfin_ref, o_ref, h_s, acc_s, *, final):
    x1 = x_ref[...] + _dot(a_ref[...], wo_ref[...]) + bo_ref[...]
    h_s[...] = _rms(x1, gf_ref[...]).astype(BF16)
    acc_s[...] = x1

    def chunk(c, _):
        uv = _dot(h_s[...], w13_ref[c])
        u = uv[:, :FFN_CHUNK]
        t = (u * _sigmoid(u) * uv[:, FFN_CHUNK:]).astype(BF16)
        acc_s[...] += _dot(t, w2_ref[c])
        return 0

    lax.fori_loop(0, N_FFN_CHUNKS, chunk, 0)
    x2 = acc_s[...]
    gate = _sigmoid(_dot(_rms(x2, gp_ref[...]).astype(BF16), wg_ref[...]))
    x3 = x2 + gate * _dot(p_ref[...].astype(BF16), wp_ref[...])
    o_ref[...] = _rms(x3, gfin_ref[...]) if final else x3


def _post(x, a, p, wo, bo, gf, w13, w2, gp, wg, wp, gfin, *, final):
    rows = x.shape[0]
    tm = _row_tile(rows)
    row = lambda i: (i, 0)
    return pl.pallas_call(
        functools.partial(_post_body, final=final),
        grid=(rows // tm,),
        in_specs=[pl.BlockSpec((tm, D), row), pl.BlockSpec((tm, D), row), pl.BlockSpec((tm, PLE_DIM), row),
                  _resident((D, D)), _resident((1, D)), _resident((1, D)),
                  _resident((N_FFN_CHUNKS, D, 2 * FFN_CHUNK)), _resident((N_FFN_CHUNKS, FFN_CHUNK, D)),
                  _resident((1, D)), _resident((D, D)), _resident((PLE_DIM, D)), _resident((1, D))],
        out_specs=pl.BlockSpec((tm, D), row),
        out_shape=jax.ShapeDtypeStruct((rows, D), F32),
        scratch_shapes=[pltpu.VMEM((tm, D), BF16), pltpu.VMEM((tm, D), F32)],
        compiler_params=_params(("parallel",)),
        name="post_mixer",
    )(x, a, p, wo, bo.reshape(1, D), gf.reshape(1, D), w13, w2, gp.reshape(1, D), wg, wp,
      gfin.reshape(1, D))


def _ffn_weights(w1, w3, w2):
    split = lambda w: w.reshape(D, N_FFN_CHUNKS, FFN_CHUNK).transpose(1, 0, 2)
    w13 = jnp.concatenate([split(w1), split(w3)], axis=-1).astype(BF16)
    return w13, w2.reshape(N_FFN_CHUNKS, FFN_CHUNK, D).astype(BF16)


def _with_cache(cache, new, total):
    b, past, w = cache.shape
    pad = jnp.zeros((b, total - past - new.shape[1], w), new.dtype)
    return jnp.concatenate([cache.astype(new.dtype), new, pad], axis=1)


def kernel(x_prompt, x_sample, p_prompt, p_sample, cache_sb_k, cache_sb_v, cache_mla_ckv, cache_mla_kpe, state_conv, cache_band_k, cache_band_v, norm_mix, norm_ffn, norm_ple, norm_final, ffn_w1, ffn_w3, ffn_w2, ple_proj, ple_gate, sb_wqkv, sb_wo, mla_wdq, mla_gq, mla_wuq, mla_wdkv, mla_gkv, mla_wuk, mla_wuv, mla_wo, conv_w1, conv_b1, conv_dw, conv_bdw, conv_ln_g, conv_ln_b, conv_w2, conv_b2, band_wqkv, band_rel_bias, band_wo):
    bp, s, _ = x_prompt.shape
    bs, t, _ = x_sample.shape
    depth = norm_mix.shape[0]
    past = cache_sb_k.shape[2]
    xp = x_prompt.reshape(bp * s, D)
    xs = x_sample.reshape(bs * t, D)
    pos_p = jnp.arange(s, dtype=jnp.int32)
    pos_s = past + jnp.arange(t, dtype=jnp.int32)
    zero_bias = jnp.zeros((D,), F32)
    head_scale = HEAD_DIM ** -0.5
    sample_keys = -(-(past + t) // KEY_TILE) * KEY_TILE
    mla_keys = -(-(past + t) // SM_KEY_TILE) * SM_KEY_TILE

    outs = {name: [] for name in ("sb_kp", "sb_vp", "sb_ks", "sb_vs", "mla_cp", "mla_pp", "mla_cs", "mla_ps",
                                  "conv_p", "conv_s", "band_kp", "band_vp", "band_ks", "band_vs")}

    for i in range(depth):
        m, j = i % N_MIXERS, i // N_MIXERS
        if m == 0:
            w = sb_wqkv[j].astype(BF16)
            q, kf, vf, kb, vb = _qkv(xp, norm_mix[i], w, head_scale)
            ap = _sb_attn(q.reshape(bp, s, D), kb.reshape(bp, s, D), vb.reshape(bp, s, D),
                          tq=ROW_TILE, q_off=0)
            outs["sb_kp"].append(kf.reshape(bp, s, HEADS, HEAD_DIM))
            outs["sb_vp"].append(vf.reshape(bp, s, HEADS, HEAD_DIM))
            q, kf, vf, kb, vb = _qkv(xs, norm_mix[i], w, head_scale)
            k_all = _with_cache(cache_sb_k[j].reshape(bs, past, D), kb.reshape(bs, t, D), sample_keys)
            v_all = _with_cache(cache_sb_v[j].reshape(bs, past, D), vb.reshape(bs, t, D), sample_keys)
            as_ = _sb_attn(q.reshape(bs, t, D), k_all, v_all, tq=t, q_off=past // t)
            outs["sb_ks"].append(kf.reshape(bs, t, HEADS, HEAD_DIM))
            outs["sb_vs"].append(vf.reshape(bs, t, HEADS, HEAD_DIM))
            wo, bo = sb_wo[j].astype(BF16), zero_bias
        elif m == 1:
            wq, wkv, wexp = _mla_weights(mla_wuq[j], mla_wdkv[j], mla_wuk[j])
            wdq, wuv = mla_wdq[j].astype(BF16), mla_wuv[j].astype(BF16)
            q, ckv, kpe = _mla_proj(xp, norm_mix[i], wdq, mla_gq[j], wq, wkv, mla_gkv[j],
                                    _rope_tables(pos_p), s)
            k, v = _mla_expand(ckv, kpe, wexp, wuv)
            ap = _softmax_attn(q.reshape(bp, s, -1), k.reshape(bp, s, -1), v.reshape(bp, s, D),
                               tq=ROW_TILE, q_off=0)
            outs["mla_cp"].append(ckv.reshape(bp, s, MLA_KV_RANK))
            outs["mla_pp"].append(kpe[:, :MLA_ROPE].reshape(bp, s, MLA_ROPE))
            q, ckv, kpe = _mla_proj(xs, norm_mix[i], wdq, mla_gq[j], wq, wkv, mla_gkv[j],
                                    _rope_tables(pos_s), t)
            ckv_all = _with_cache(cache_mla_ckv[j], ckv.reshape(bs, t, MLA_KV_RANK), mla_keys)
            kpe_cache = jnp.pad(cache_mla_kpe[j], ((0, 0), (0, 0), (0, LANES - MLA_ROPE)))
            kpe_all = _with_cache(kpe_cache, kpe.reshape(bs, t, LANES), mla_keys)
            k, v = _mla_expand(ckv_all.reshape(bs * mla_keys, MLA_KV_RANK),
                               kpe_all.reshape(bs * mla_keys, LANES), wexp, wuv)
            as_ = _softmax_attn(q.reshape(bs, t, -1), k.reshape(bs, mla_keys, -1),
                                v.reshape(bs, mla_keys, D), tq=t, q_off=past // t)
            outs["mla_cs"].append(ckv.reshape(bs, t, MLA_KV_RANK))
            outs["mla_ps"].append(kpe[:, :MLA_ROPE].reshape(bs, t, MLA_ROPE))
            wo, bo = mla_wo[j].astype(BF16), zero_bias
        elif m == 2:
            w1 = conv_w1[j].astype(BF16)
            ap, st = _conv(xp.reshape(bp, s, D), norm_mix[i], w1, conv_b1[j],
                           jnp.zeros((bp, CONV_HIST, D), F32), conv_dw[j], conv_bdw[j],
                           conv_ln_g[j], conv_ln_b[j])
            outs["conv_p"].append(st)
            as_, st = _conv(xs.reshape(bs, t, D), norm_mix[i], w1, conv_b1[j], state_conv[j],
                            conv_dw[j], conv_bdw[j], conv_ln_g[j], conv_ln_b[j])
            outs["conv_s"].append(st)
            wo, bo = conv_w2[j].astype(BF16), conv_b2[j]
        else:
            w = band_wqkv[j].astype(BF16)
            bias_tiles = _band_bias_tiles(band_rel_bias[j])
            q, kf, vf, kb, vb = _qkv(xp, norm_mix[i], w, head_scale * LOG2E)
            ap = _band_attn(q.reshape(bp, s, D), kb.reshape(bp, s, D), vb.reshape(bp, s, D),
                            bias_tiles, q_off=0)
            keep = min(BAND_PAST, s)
            outs["band_kp"].append(kf.reshape(bp, s, HEADS, HEAD_DIM)[:, s - keep:])
            outs["band_vp"].append(vf.reshape(bp, s, HEADS, HEAD_DIM)[:, s - keep:])
            q, kf, vf, kb, vb = _qkv(xs, norm_mix[i], w, head_scale * LOG2E)
            n_past = cache_band_k.shape[2]
            local = n_past + BAND_TQ
            k_all = _with_cache(cache_band_k[j].reshape(bs, n_past, D), kb.reshape(bs, t, D), local)
            v_all = _with_cache(cache_band_v[j].reshape(bs, n_past, D), vb.reshape(bs, t, D), local)
            q_pad = jnp.pad(q.reshape(bs, t, D), ((0, 0), (0, BAND_TQ - t), (0, 0)))
            as_ = _band_attn(q_pad, k_all, v_all, bias_tiles, q_off=n_past // BAND_TQ)[:, :t]
            kk = jnp.concatenate([cache_band_k[j], kf.reshape(bs, t, HEADS, HEAD_DIM)], axis=1)
            vv = jnp.concatenate([cache_band_v[j], vf.reshape(bs, t, HEADS, HEAD_DIM)], axis=1)
            outs["band_ks"].append(kk[:, t:])
            outs["band_vs"].append(vv[:, t:])
            wo, bo = band_wo[j].astype(BF16), zero_bias

        w13, w2 = _ffn_weights(ffn_w1[i], ffn_w3[i], ffn_w2[i])
        final = i == depth - 1
        post = functools.partial(_post, wo=wo, bo=bo, gf=norm_ffn[i], w13=w13, w2=w2, gp=norm_ple[i],
                                 wg=ple_gate[i].astype(BF16), wp=ple_proj[i].astype(BF16),
                                 gfin=norm_final, final=final)
        xp = post(xp, ap.reshape(bp * s, D), p_prompt[i].reshape(bp * s, PLE_DIM))
        xs = post(xs, as_.reshape(bs * t, D), p_sample[i].reshape(bs * t, PLE_DIM))

    stack = lambda name: jnp.stack(outs[name])
    return (xp.reshape(bp, s, D), xs.reshape(bs, t, D),
            stack("sb_kp"), stack("sb_vp"), stack("sb_ks"), stack("sb_vs"),
            stack("mla_cp"), stack("mla_pp"), stack("mla_cs"), stack("mla_ps"),
            stack("conv_p"), stack("conv_s"),
            stack("band_kp"), stack("band_vp"), stack("band_ks"), stack("band_vs"))
```

```python
import functools
import math

import jax
import jax.numpy as jnp
from jax import lax
from jax.experimental import pallas as pl
from jax.experimental.pallas import tpu as pltpu

F32 = jnp.float32
BF16 = jnp.bfloat16

D = 1024
EPS = 1e-6
NEG = -1e30
HEADS = 16
HEAD_DIM = 64
CHUNK = 64
PLE_DIM = 256
FFN_HIDDEN = 2816
FFN_CHUNK = 256
N_FFN_CHUNKS = FFN_HIDDEN // FFN_CHUNK
MLA_Q_RANK = 256
MLA_KV_RANK = 128
MLA_NOPE = 64
MLA_ROPE = 32
MLA_SCALE = (MLA_NOPE + MLA_ROPE) ** -0.5
ROPE_THETA = 10000.0
CONV_WIDTH = 31
CONV_HIST = CONV_WIDTH - 1
HIST_PAD = 32
BAND_CHUNKS = 8
BAND_PAST = BAND_CHUNKS * CHUNK
REL_CLIP = 128
N_MIXERS = 4

LANES = 128
KEY_TILE = 256
SM_KEY_TILE = 2 * KEY_TILE
SM_HEADS = 2
LOG2E = math.log2(math.e)
ROW_TILE = 512
VMEM_LIMIT = 56 * 1024 * 1024


def _dot(a, b):
    return jnp.dot(a, b, preferred_element_type=F32)


def _dot_nt(a, b):
    return lax.dot_general(a, b, (((1,), (1,)), ((), ())), preferred_element_type=F32)


def _rms(x, g):
    return x * lax.rsqrt(jnp.mean(x * x, axis=-1, keepdims=True) + EPS) * g


def _sigmoid(x):
    return 1.0 / (1.0 + jnp.exp(-x))


def _params(sem):
    return pltpu.CompilerParams(dimension_semantics=sem, vmem_limit_bytes=VMEM_LIMIT)


def _resident(shape):
    zeros = (0,) * len(shape)
    return pl.BlockSpec(shape, lambda *_: zeros, pipeline_mode=pl.Buffered(1))


def _row_tile(rows):
    return min(ROW_TILE, rows)


def _chunk_of(idx):
    return lax.shift_right_arithmetic(idx, CHUNK.bit_length() - 1)


def _head_lanes(lane, h):
    return lane < HEAD_DIM if h == 0 else lane >= HEAD_DIM


def _rows_by_head(rows):
    lane = lax.broadcasted_iota(jnp.int32, rows.shape, 1)
    zero = jnp.zeros_like(rows)
    return jnp.concatenate([jnp.where(_head_lanes(lane, h), rows, zero) for h in range(2)], axis=0)


def _qkv_body(x_ref, g_ref, w_ref, q_ref, kf_ref, vf_ref, kb_ref, vb_ref, *, scale):
    h = _rms(x_ref[...], g_ref[...]).astype(BF16)
    q_ref[...] = (_dot(h, w_ref[:, 0:D]) * scale).astype(BF16)
    k = _dot(h, w_ref[:, D:2 * D])
    kf_ref[...] = k
    kb_ref[...] = k.astype(BF16)
    v = _dot(h, w_ref[:, 2 * D:3 * D])
    vf_ref[...] = v
    vb_ref[...] = v.astype(BF16)


def _qkv(x, g, w, scale):
    rows = x.shape[0]
    tm = _row_tile(rows)
    row = lambda i: (i, 0)
    out_sds = lambda dt: jax.ShapeDtypeStruct((rows, D), dt)
    return pl.pallas_call(
        functools.partial(_qkv_body, scale=scale),
        grid=(rows // tm,),
        in_specs=[pl.BlockSpec((tm, D), row), _resident((1, D)), _resident((D, 3 * D))],
        out_specs=[pl.BlockSpec((tm, D), row)] * 5,
        out_shape=[out_sds(BF16), out_sds(F32), out_sds(F32), out_sds(BF16), out_sds(BF16)],
        compiler_params=_params(("parallel",)),
        name="qkv_proj",
    )(x, g.reshape(1, D), w)


def _sb_attn_body(q_ref, k_ref, v_ref, tri_ref, o_ref, *, tq, q_off):
    q0 = (pl.program_id(2) + q_off) * tq
    first_diag = lax.div(q0, KEY_TILE)
    n_diag = max(1, tq // KEY_TILE)
    lane = lax.broadcasted_iota(jnp.int32, (tq, LANES), 1)
    diff = (lax.broadcasted_iota(jnp.int32, (tq, KEY_TILE), 1)
            - lax.broadcasted_iota(jnp.int32, (tq, KEY_TILE), 0))
    col_minus_row = jnp.concatenate([diff, diff], axis=1)
    q_pair = q_ref[0]
    tri = tri_ref[...]

    def tile(j, masked):
        k0 = pl.multiple_of(j * KEY_TILE, KEY_TILE)
        z = _dot_nt(q_pair, _rows_by_head(k_ref[0, pl.ds(k0, KEY_TILE), :]))
        neg_abs = pltpu.bitcast(pltpu.bitcast(z, jnp.uint32) | jnp.uint32(0x80000000), F32)
        drop = jnp.maximum(z, 0.0) + jnp.log(1.0 + jnp.exp2(neg_abs)) * LOG2E
        if masked:
            valid = col_minus_row < (q0 - k0)
            drop = jnp.where(valid, drop, 0.0)
        hi = drop.astype(BF16)
        later = jnp.concatenate([_dot(hi[:, :KEY_TILE], tri), _dot(hi[:, KEY_TILE:], tri)], axis=1)
        w = jnp.exp2((z - drop) - later)
        if masked:
            w = jnp.where(valid, w, 0.0)
        o = _dot(w.astype(BF16), _rows_by_head(v_ref[0, pl.ds(k0, KEY_TILE), :]))
        total = jnp.where(lane < HEAD_DIM, jnp.sum(drop[:, :KEY_TILE], axis=-1, keepdims=True),
                          jnp.sum(drop[:, KEY_TILE:], axis=-1, keepdims=True))
        return o, total

    def step(tiles, carry, masked):
        dropped, out = carry
        for j in tiles:
            o, total = tile(j, masked)
            out = out + jnp.exp2(-dropped) * o
            dropped = dropped + total
        return dropped, out

    carry = (jnp.zeros((tq, LANES), F32), jnp.zeros((tq, LANES), F32))
    carry = step([first_diag + d for d in reversed(range(n_diag))], carry, True)
    carry = lax.fori_loop(
        0, lax.div(first_diag, 2),
        lambda t, c: step([first_diag - 1 - 2 * t, first_diag - 2 - 2 * t], c, False), carry)
    o_ref[0] = carry[1].astype(BF16)


def _tri():
    j = lax.broadcasted_iota(jnp.int32, (KEY_TILE, KEY_TILE), 0)
    s = lax.broadcasted_iota(jnp.int32, (KEY_TILE, KEY_TILE), 1)
    return (j > s).astype(BF16)


def _sb_attn(q, k, v, *, tq, q_off):
    b, sq, _ = q.shape
    sk = k.shape[1]
    assert (q_off * tq) % (2 * KEY_TILE) == 0 and (tq % (2 * KEY_TILE) == 0 or sq == tq)
    assert (q_off + sq // tq - 1) * tq + max(tq, KEY_TILE) <= sk
    return pl.pallas_call(
        functools.partial(_sb_attn_body, tq=tq, q_off=q_off),
        grid=(b, HEADS // 2, sq // tq),
        in_specs=[pl.BlockSpec((1, tq, LANES), lambda bi, p, qi: (bi, qi, p)),
                  pl.BlockSpec((1, sk, LANES), lambda bi, p, qi: (bi, 0, p)),
                  pl.BlockSpec((1, sk, LANES), lambda bi, p, qi: (bi, 0, p)),
                  pl.BlockSpec((KEY_TILE, KEY_TILE), lambda bi, p, qi: (0, 0))],
        out_specs=pl.BlockSpec((1, tq, LANES), lambda bi, p, qi: (bi, qi, p)),
        out_shape=jax.ShapeDtypeStruct((b, sq, D), BF16),
        compiler_params=_params(("parallel", "parallel", "arbitrary")),
        name="sb_attn",
    )(q, k, v, _tri())


def _value_with_ones(v_tile, h):
    lane = lax.broadcasted_iota(jnp.int32, v_tile.shape, 1)
    return jnp.where(_head_lanes(lane, h), v_tile, jnp.ones_like(v_tile))


def _normalise_pair(accs):
    res = [a * pltpu.roll(1.0 / a, HEAD_DIM, 1) for a in accs]
    lane = lax.broadcasted_iota(jnp.int32, res[0].shape, 1)
    return jnp.where(lane < HEAD_DIM, res[0], res[1])


def _softmax_attn_body(q_ref, k_ref, v_ref, o_ref, *, tq, q_off):
    q0 = (pl.program_id(2) + q_off) * tq
    last = lax.div(q0, SM_KEY_TILE)
    chunk_diff = (_chunk_of(lax.broadcasted_iota(jnp.int32, (tq, SM_KEY_TILE), 1))
                  - _chunk_of(lax.broadcasted_iota(jnp.int32, (tq, SM_KEY_TILE), 0)))

    def step(tiles, carry, masked):
        new = list(carry)
        for j in tiles:
            k0 = pl.multiple_of(j * SM_KEY_TILE, SM_KEY_TILE)
            for h in range(SM_HEADS):
                m, acc = new[h]
                s = _dot_nt(q_ref[0, :, LANES * h:LANES * (h + 1)],
                            k_ref[0, pl.ds(k0, SM_KEY_TILE), LANES * h:LANES * (h + 1)])
                if masked:
                    s = jnp.where(chunk_diff <= _chunk_of(q0 - k0), s, NEG)
                m_new = jnp.maximum(m, jnp.max(s, axis=-1, keepdims=True))
                p = jnp.exp2(s - m_new)
                v_pair = v_ref[0, pl.ds(k0, SM_KEY_TILE), LANES * (h // 2):LANES * (h // 2 + 1)]
                acc = acc * jnp.exp2(m - m_new) + _dot(p.astype(BF16), _value_with_ones(v_pair, h % 2))
                new[h] = (m_new, acc)
        return tuple(new)

    carry = tuple((jnp.full((tq, 1), NEG, F32), jnp.zeros((tq, LANES), F32)) for _ in range(SM_HEADS))
    pairs = lax.div(last, 2)
    carry = lax.fori_loop(0, pairs, lambda t, c: step([2 * t, 2 * t + 1], c, False), carry)
    carry = lax.fori_loop(2 * pairs, last, lambda t, c: step([t], c, False), carry)
    carry = step([last], carry, True)
    for pair in range(SM_HEADS // 2):
        o_ref[0, :, LANES * pair:LANES * (pair + 1)] = _normalise_pair(
            [carry[2 * pair][1], carry[2 * pair + 1][1]]).astype(BF16)


def _softmax_attn(q, k, v, *, tq, q_off):
    b, sq, _ = q.shape
    sk = k.shape[1]
    assert tq <= SM_KEY_TILE and SM_KEY_TILE % tq == 0 and sk % SM_KEY_TILE == 0
    qk_lanes, v_lanes = SM_HEADS * LANES, SM_HEADS * HEAD_DIM
    return pl.pallas_call(
        functools.partial(_softmax_attn_body, tq=tq, q_off=q_off),
        grid=(b, HEADS // SM_HEADS, sq // tq),
        in_specs=[pl.BlockSpec((1, tq, qk_lanes), lambda bi, p, qi: (bi, qi, p)),
                  pl.BlockSpec((1, sk, qk_lanes), lambda bi, p, qi: (bi, 0, p)),
                  pl.BlockSpec((1, sk, v_lanes), lambda bi, p, qi: (bi, 0, p))],
        out_specs=pl.BlockSpec((1, tq, v_lanes), lambda bi, p, qi: (bi, qi, p)),
        out_shape=jax.ShapeDtypeStruct((b, sq, D), BF16),
        compiler_params=_params(("parallel", "parallel", "arbitrary")),
        name="mla_attn",
    )(q, k, v)


BAND_TQ = 4 * CHUNK


BAND_KEYS = 3 * KEY_TILE
BAND_BLOCKS = 4


def _band_attn_body(q_ref, k_ref, v_ref, bias_ref, o_ref, *, q_off, blocks):
    row = lax.broadcasted_iota(jnp.int32, (2 * BAND_KEYS, LANES), 0)
    lane = lax.broadcasted_iota(jnp.int32, (2 * BAND_KEYS, LANES), 1)
    ones = jnp.where((row < BAND_KEYS) == (lane < HEAD_DIM), 1.0, 0.0).astype(BF16)
    for g in range(blocks):
        rows = slice(BAND_TQ * g, BAND_TQ * (g + 1))
        qi = pl.program_id(2) * blocks + g + q_off
        k0 = pl.multiple_of(jnp.maximum(qi - 2, 0) * KEY_TILE, KEY_TILE)
        k_win = k_ref[0, pl.ds(k0, BAND_KEYS), :]
        v_win = v_ref[0, pl.ds(k0, BAND_KEYS), :]
        s = _dot_nt(q_ref[0, rows, :], _rows_by_head(k_win)) + bias_ref[0, jnp.minimum(qi, 2)]
        p = jnp.concatenate(
            [jnp.exp2(sh - jnp.max(sh, axis=-1, keepdims=True)) for sh in (s[:, :BAND_KEYS], s[:, BAND_KEYS:])],
            axis=1).astype(BF16)
        acc = _dot(p, jnp.concatenate([_rows_by_head(v_win), ones], axis=1))
        o_ref[0, rows, :] = (acc[:, :LANES] * (1.0 / acc[:, LANES:])).astype(BF16)


def _band_bias_tiles(rel_bias):
    table = rel_bias.astype(F32) * LOG2E
    heads = table.shape[0]
    n = BAND_TQ + KEY_TILE - 1
    tiles = []
    for t in range(3):
        d_lo = BAND_PAST - KEY_TILE * t - (KEY_TILE - 1)
        below = max(0, min(n, -REL_CLIP - d_lo))
        above = max(0, min(n, d_lo + n - 1 - REL_CLIP))
        mid_lo = d_lo + below + REL_CLIP
        diag = jnp.concatenate([jnp.broadcast_to(table[:, :1], (heads, below)),
                                table[:, mid_lo:mid_lo + n - below - above],
                                jnp.broadcast_to(table[:, -1:], (heads, above))], axis=1)
        rows = jnp.tile(diag, (1, BAND_TQ + 1))[:, :BAND_TQ * (n + 1)].reshape(heads, BAND_TQ, n + 1)
        tiles.append(rows[:, :, KEY_TILE - 1::-1])
    return jnp.stack(tiles, axis=1)


def _band_bias_layouts(rel_bias):
    tiles = _band_bias_tiles(rel_bias)
    heads = tiles.shape[0]
    row_chunk = (jnp.arange(BAND_TQ, dtype=jnp.int32) // CHUNK)[:, None]
    col_chunk = (jnp.arange(KEY_TILE, dtype=jnp.int32) // CHUNK)[None, :]
    eight_back = jnp.where(row_chunk <= col_chunk, tiles[:, 0], NEG)
    four_back = tiles[:, 1]
    own = jnp.where(col_chunk <= row_chunk, tiles[:, 2], NEG)
    ahead = jnp.full_like(own, NEG)
    layouts = [jnp.concatenate(slots, axis=-1) for slots in
               ((own, ahead, ahead), (four_back, own, ahead), (eight_back, four_back, own))]
    per_head = jnp.stack(layouts, axis=1)
    pairs = per_head.reshape(heads // 2, 2, 3, BAND_TQ, BAND_KEYS)
    return jnp.concatenate([pairs[:, 0], pairs[:, 1]], axis=-1)


def _band_attn(q, k, v, bias_layouts, *, q_off):
    b, sq, _ = q.shape
    sk = k.shape[1]
    assert sk >= BAND_KEYS and (q_off + sq // BAND_TQ) * BAND_TQ <= sk
    blocks = math.gcd(sq // BAND_TQ, BAND_BLOCKS)
    rows = blocks * BAND_TQ
    return pl.pallas_call(
        functools.partial(_band_attn_body, q_off=q_off, blocks=blocks),
        grid=(HEADS // 2, b, sq // rows),
        in_specs=[pl.BlockSpec((1, rows, LANES), lambda p, bi, qi: (bi, qi, p)),
                  pl.BlockSpec((1, sk, LANES), lambda p, bi, qi: (bi, 0, p)),
                  pl.BlockSpec((1, sk, LANES), lambda p, bi, qi: (bi, 0, p)),
                  pl.BlockSpec((1, 3, BAND_TQ, 2 * BAND_KEYS), lambda p, bi, qi: (p, 0, 0, 0))],
        out_specs=pl.BlockSpec((1, rows, LANES), lambda p, bi, qi: (bi, qi, p)),
        out_shape=jax.ShapeDtypeStruct((b, sq, D), BF16),
        compiler_params=_params(("parallel", "parallel", "arbitrary")),
        name="band_attn",
    )(q, k, v, bias_layouts)


def _mla_proj_body(x_ref, g_ref, wdq_ref, gq_ref, wq_ref, wkv_ref, gkv_ref,
                   qcos_ref, qsin_ref, kcos_ref, ksin_ref, q_ref, ckv_ref, kpe_ref):
    h = _rms(x_ref[...], g_ref[...]).astype(BF16)
    cq = _rms(_dot(h, wdq_ref[...]), gq_ref[...]).astype(BF16)
    qq = _dot(cq, wq_ref[...])
    qcos, qsin = qcos_ref[...], qsin_ref[...]
    half = HEADS * LANES
    for hd in range(HEADS):
        lo = LANES * hd
        q_ref[:, lo:lo + LANES] = (qq[:, lo:lo + LANES] * qcos
                                   + qq[:, half + lo:half + lo + LANES] * qsin).astype(BF16)
    kv = _dot(h, wkv_ref[...])
    ckv_ref[...] = _rms(kv[:, 0:LANES], gkv_ref[...])
    kpe_ref[...] = kv[:, LANES:2 * LANES] * kcos_ref[...] + kv[:, 2 * LANES:3 * LANES] * ksin_ref[...]


def _rope_tables(pos):
    half = MLA_ROPE // 2
    inv = ROPE_THETA ** (-jnp.arange(half, dtype=F32) / half)
    ang = pos.astype(F32)[:, None] * inv[None, :]
    cos = jnp.concatenate([jnp.cos(ang)] * 2, axis=-1)
    sin = jnp.concatenate([jnp.sin(ang)] * 2, axis=-1)
    n = pos.shape[0]
    pad = jnp.zeros((n, LANES - MLA_NOPE - MLA_ROPE), F32)
    q_scale = MLA_SCALE * LOG2E
    qcos = jnp.concatenate([jnp.ones((n, MLA_NOPE), F32), cos, pad], axis=-1) * q_scale
    qsin = jnp.concatenate([jnp.zeros((n, MLA_NOPE), F32), sin, pad], axis=-1) * q_scale
    kpad = jnp.zeros((n, LANES - MLA_ROPE), F32)
    kcos = jnp.concatenate([cos, kpad], axis=-1)
    ksin = jnp.concatenate([sin, kpad], axis=-1)
    return qcos, qsin, kcos, ksin


def _rotary_partner(w):
    half = MLA_ROPE // 2
    return jnp.concatenate([-w[..., half:], w[..., :half]], axis=-1)


def _mla_weights(wuq, wdkv, wuk):
    r = wuq.shape[0]
    wq3 = wuq.reshape(r, HEADS, MLA_NOPE + MLA_ROPE)
    zpad = jnp.zeros((r, HEADS, LANES - MLA_NOPE - MLA_ROPE), wuq.dtype)
    plain = jnp.concatenate([wq3, zpad], axis=-1).reshape(r, HEADS * LANES)
    partner = jnp.concatenate([jnp.zeros((r, HEADS, MLA_NOPE), wuq.dtype),
                               _rotary_partner(wq3[..., MLA_NOPE:]), zpad], axis=-1)
    wq = jnp.concatenate([plain, partner.reshape(r, HEADS * LANES)], axis=-1).astype(BF16)
    rope_w = wdkv[:, MLA_KV_RANK:]
    kz = jnp.zeros((D, LANES - MLA_ROPE), wdkv.dtype)
    wkv = jnp.concatenate([wdkv[:, :MLA_KV_RANK], rope_w, kz, _rotary_partner(rope_w), kz],
                          axis=-1).astype(BF16)
    top = jnp.concatenate([wuk.reshape(MLA_KV_RANK, HEADS, MLA_NOPE),
                           jnp.zeros((MLA_KV_RANK, HEADS, LANES - MLA_NOPE), wuk.dtype)], axis=-1)
    place = jnp.concatenate([jnp.zeros((MLA_ROPE, MLA_NOPE), F32), jnp.eye(MLA_ROPE, dtype=F32),
                             jnp.zeros((MLA_ROPE, LANES - MLA_NOPE - MLA_ROPE), F32)], axis=-1)
    place = jnp.concatenate([place, jnp.zeros((LANES - MLA_ROPE, LANES), F32)], axis=0)
    bottom = jnp.broadcast_to(place[:, None, :], (LANES, HEADS, LANES))
    wexp = jnp.concatenate([top, bottom], axis=0).reshape(2 * LANES, HEADS * LANES).astype(BF16)
    return wq, wkv, wexp


def _mla_proj(x, g, wdq, gq, wq, wkv, gkv, tables, seq):
    rows = x.shape[0]
    tm = _row_tile(rows)
    if seq >= tm:
        n_pos_tiles = seq // tm
        pos_map = lambda i: (i % n_pos_tiles, 0)
    else:
        tables = [jnp.tile(t, (tm // seq, 1)) for t in tables]
        pos_map = lambda i: (0, 0)
    row = lambda i: (i, 0)
    return pl.pallas_call(
        _mla_proj_body,
        grid=(rows // tm,),
        in_specs=[pl.BlockSpec((tm, D), row), _resident((1, D)), _resident((D, MLA_Q_RANK)),
                  _resident((1, MLA_Q_RANK)), _resident((MLA_Q_RANK, 2 * HEADS * LANES)),
                  _resident((D, 3 * LANES)), _resident((1, MLA_KV_RANK))]
                 + [pl.BlockSpec((tm, LANES), pos_map)] * 4,
        out_specs=[pl.BlockSpec((tm, HEADS * LANES), row), pl.BlockSpec((tm, LANES), row),
                   pl.BlockSpec((tm, LANES), row)],
        out_shape=[jax.ShapeDtypeStruct((rows, HEADS * LANES), BF16),
                   jax.ShapeDtypeStruct((rows, MLA_KV_RANK), F32),
                   jax.ShapeDtypeStruct((rows, LANES), F32)],
        compiler_params=_params(("parallel",)),
        name="mla_proj",
    )(x, g.reshape(1, D), wdq, gq.reshape(1, MLA_Q_RANK), wq, wkv, gkv.reshape(1, MLA_KV_RANK), *tables)


def _mla_expand_body(ckv_ref, kpe_ref, wexp_ref, wuv_ref, k_ref, v_ref):
    c = ckv_ref[...].astype(BF16)
    cat = jnp.concatenate([c, kpe_ref[...].astype(BF16)], axis=-1)
    k_ref[...] = _dot(cat, wexp_ref[...]).astype(BF16)
    v_ref[...] = _dot(c, wuv_ref[...]).astype(BF16)


def _mla_expand(ckv, kpe, wexp, wuv):
    rows = ckv.shape[0]
    tm = _row_tile(rows)
    row = lambda i: (i, 0)
    return pl.pallas_call(
        _mla_expand_body,
        grid=(rows // tm,),
        in_specs=[pl.BlockSpec((tm, LANES), row), pl.BlockSpec((tm, LANES), row),
                  _resident((2 * LANES, HEADS * LANES)), _resident((MLA_KV_RANK, D))],
        out_specs=[pl.BlockSpec((tm, HEADS * LANES), row), pl.BlockSpec((tm, D), row)],
        out_shape=[jax.ShapeDtypeStruct((rows, HEADS * LANES), BF16),
                   jax.ShapeDtypeStruct((rows, D), BF16)],
        compiler_params=_params(("parallel",)),
        name="mla_expand",
    )(ckv, kpe, wexp, wuv)


CONV_ROWS = 32
LANE_COLS = D // LANES


def _conv_body(x_ref, g_ref, w1_ref, b1_ref, hist_ref, dw_ref, bdw_ref, lng_ref, lnb_ref,
               a_ref, st_ref, gbuf, ybuf, *, tm):
    @pl.when(pl.program_id(1) == 0)
    def _():
        for c in range(LANE_COLS):
            gbuf[c, 0:HIST_PAD, :] = hist_ref[0, :, LANES * c:LANES * (c + 1)]

    h = _rms(x_ref[0], g_ref[...]).astype(BF16)
    u = _dot(h, w1_ref[...]) + b1_ref[...]
    glu = u[:, :D] * _sigmoid(u[:, D:])
    for c in range(LANE_COLS):
        gbuf[c, HIST_PAD:HIST_PAD + tm, :] = glu[:, LANES * c:LANES * (c + 1)]

    first = HIST_PAD - CONV_HIST

    def rows(i, _):
        r0 = pl.multiple_of(i * CONV_ROWS, CONV_ROWS)
        for c in range(LANE_COLS):
            cols = slice(LANES * c, LANES * (c + 1))
            acc = jnp.broadcast_to(bdw_ref[:, cols], (CONV_ROWS, LANES))
            for w in range(CONV_WIDTH):
                acc = acc + dw_ref[w:w + 1, cols] * gbuf[c, pl.ds(r0 + first + w, CONV_ROWS), :]
            ybuf[pl.ds(r0, CONV_ROWS), cols] = acc
        return 0

    lax.fori_loop(0, tm // CONV_ROWS, rows, 0)

    y = ybuf[...]
    yc = y - jnp.mean(y, axis=-1, keepdims=True)
    var = jnp.mean(yc * yc, axis=-1, keepdims=True)
    yn = yc * lax.rsqrt(var + EPS) * lng_ref[...] + lnb_ref[...]
    a_ref[0] = (yn * _sigmoid(yn)).astype(BF16)

    for c in range(LANE_COLS):
        tail = gbuf[c, tm:tm + HIST_PAD, :]
        st_ref[0, :, LANES * c:LANES * (c + 1)] = tail
        gbuf[c, 0:HIST_PAD, :] = tail


def _conv(x, g, w1, b1, hist, dw, bdw, lng, lnb):
    b, s, _ = x.shape
    tm = _row_tile(s)
    hist = jnp.pad(hist.astype(F32), ((0, 0), (HIST_PAD - CONV_HIST, 0), (0, 0)))
    dw = jnp.pad(dw.astype(F32), ((0, HIST_PAD - CONV_WIDTH), (0, 0)))
    a, st = pl.pallas_call(
        functools.partial(_conv_body, tm=tm),
        grid=(b, s // tm),
        in_specs=[pl.BlockSpec((1, tm, D), lambda bi, t: (bi, t, 0)), _resident((1, D)),
                  _resident((D, 2 * D)), _resident((1, 2 * D)),
                  pl.BlockSpec((1, HIST_PAD, D), lambda bi, t: (bi, 0, 0)),
                  _resident((HIST_PAD, D)), _resident((1, D)), _resident((1, D)), _resident((1, D))],
        out_specs=[pl.BlockSpec((1, tm, D), lambda bi, t: (bi, t, 0)),
                   pl.BlockSpec((1, HIST_PAD, D), lambda bi, t: (bi, 0, 0))],
        out_shape=[jax.ShapeDtypeStruct((b, s, D), BF16),
                   jax.ShapeDtypeStruct((b, HIST_PAD, D), F32)],
        scratch_shapes=[pltpu.VMEM((LANE_COLS, HIST_PAD + tm, LANES), F32), pltpu.VMEM((tm, D), F32)],
        compiler_params=_params(("parallel", "arbitrary")),
        name="conv_module",
    )(x, g.reshape(1, D), w1, b1.reshape(1, 2 * D), hist, dw, bdw.reshape(1, D),
      lng.reshape(1, D), lnb.reshape(1, D))
    return a, st[:, HIST_PAD - CONV_HIST:]


def _post_body(x_ref, a_ref, p_ref, wo_ref, bo_ref, gf_ref, w13_ref, w2_ref, gp_ref, wg_ref, wp_ref,
               gfin_ref, o_ref, h_s, acc_s, *, final):
    x1 = x_ref[...] + _dot(a_ref[...], wo_ref[...]) + bo_ref[...]
    h_s[...] = _rms(x1, gf_ref[...]).astype(BF16)
    acc_s[...] = x1

    def gated(c):
        uv = _dot(h_s[...], w13_ref[c])
        u = uv[:, :FFN_CHUNK]
        return _dot((u * _sigmoid(u) * uv[:, FFN_CHUNK:]).astype(BF16), w2_ref[c])

    def chunk_pair(i, _):
        acc_s[...] += gated(2 * i) + gated(2 * i + 1)
        return 0

    lax.fori_loop(0, N_FFN_CHUNKS // 2, chunk_pair, 0)
    for c in range(N_FFN_CHUNKS // 2 * 2, N_FFN_CHUNKS):
        acc_s[...] += gated(c)
    x2 = acc_s[...]
    gate = _sigmoid(_dot(_rms(x2, gp_ref[...]).astype(BF16), wg_ref[...]))
    x3 = x2 + gate * _dot(p_ref[...].astype(BF16), wp_ref[...])
    o_ref[...] = _rms(x3, gfin_ref[...]) if final else x3


def _post(x, a, p, wo, bo, gf, w13, w2, gp, wg, wp, gfin, *, final):
    rows = x.shape[0]
    tm = _row_tile(rows)
    row = lambda i: (i, 0)
    return pl.pallas_call(
        functools.partial(_post_body, final=final),
        grid=(rows // tm,),
        in_specs=[pl.BlockSpec((tm, D), row), pl.BlockSpec((tm, D), row), pl.BlockSpec((tm, PLE_DIM), row),
                  _resident((D, D)), _resident((1, D)), _resident((1, D)),
                  _resident((N_FFN_CHUNKS, D, 2 * FFN_CHUNK)), _resident((N_FFN_CHUNKS, FFN_CHUNK, D)),
                  _resident((1, D)), _resident((D, D)), _resident((PLE_DIM, D)), _resident((1, D))],
        out_specs=pl.BlockSpec((tm, D), row),
        out_shape=jax.ShapeDtypeStruct((rows, D), F32),
        scratch_shapes=[pltpu.VMEM((tm, D), BF16), pltpu.VMEM((tm, D), F32)],
        compiler_params=_params(("parallel",)),
        name="post_mixer",
    )(x, a, p, wo, bo.reshape(1, D), gf.reshape(1, D), w13, w2, gp.reshape(1, D), wg, wp,
      gfin.reshape(1, D))


def _ffn_weights(w1, w3, w2):
    split = lambda w: w.reshape(D, N_FFN_CHUNKS, FFN_CHUNK).transpose(1, 0, 2)
    w13 = jnp.concatenate([split(w1), split(w3)], axis=-1).astype(BF16)
    return w13, w2.reshape(N_FFN_CHUNKS, FFN_CHUNK, D).astype(BF16)


def _with_cache(cache, new, total):
    b, past, w = cache.shape
    pad = jnp.zeros((b, total - past - new.shape[1], w), new.dtype)
    return jnp.concatenate([cache.astype(new.dtype), new, pad], axis=1)


def kernel(x_prompt, x_sample, p_prompt, p_sample, cache_sb_k, cache_sb_v, cache_mla_ckv, cache_mla_kpe, state_conv, cache_band_k, cache_band_v, norm_mix, norm_ffn, norm_ple, norm_final, ffn_w1, ffn_w3, ffn_w2, ple_proj, ple_gate, sb_wqkv, sb_wo, mla_wdq, mla_gq, mla_wuq, mla_wdkv, mla_gkv, mla_wuk, mla_wuv, mla_wo, conv_w1, conv_b1, conv_dw, conv_bdw, conv_ln_g, conv_ln_b, conv_w2, conv_b2, band_wqkv, band_rel_bias, band_wo):
    bp, s, _ = x_prompt.shape
    bs, t, _ = x_sample.shape
    depth = norm_mix.shape[0]
    past = cache_sb_k.shape[2]
    xp = x_prompt.reshape(bp * s, D)
    xs = x_sample.reshape(bs * t, D)
    pos_p = jnp.arange(s, dtype=jnp.int32)
    pos_s = past + jnp.arange(t, dtype=jnp.int32)
    zero_bias = jnp.zeros((D,), F32)
    head_scale = HEAD_DIM ** -0.5 * LOG2E
    sample_keys = -(-(past + t) // KEY_TILE) * KEY_TILE
    mla_keys = -(-(past + t) // SM_KEY_TILE) * SM_KEY_TILE

    outs = {name: [] for name in ("sb_kp", "sb_vp", "sb_ks", "sb_vs", "mla_cp", "mla_pp", "mla_cs", "mla_ps",
                                  "conv_p", "conv_s", "band_kp", "band_vp", "band_ks", "band_vs")}

    for i in range(depth):
        m, j = i % N_MIXERS, i // N_MIXERS
        if m == 0:
            w = sb_wqkv[j].astype(BF16)
            q, kf, vf, kb, vb = _qkv(xp, norm_mix[i], w, head_scale)
            ap = _sb_attn(q.reshape(bp, s, D), kb.reshape(bp, s, D), vb.reshape(bp, s, D),
                          tq=ROW_TILE, q_off=0)
            outs["sb_kp"].append(kf.reshape(bp, s, HEADS, HEAD_DIM))
            outs["sb_vp"].append(vf.reshape(bp, s, HEADS, HEAD_DIM))
            q, kf, vf, kb, vb = _qkv(xs, norm_mix[i], w, head_scale)
            k_all = _with_cache(cache_sb_k[j].reshape(bs, past, D), kb.reshape(bs, t, D), sample_keys)
            v_all = _with_cache(cache_sb_v[j].reshape(bs, past, D), vb.reshape(bs, t, D), sample_keys)
            as_ = _sb_attn(q.reshape(bs, t, D), k_all, v_all, tq=t, q_off=past // t)
            outs["sb_ks"].append(kf.reshape(bs, t, HEADS, HEAD_DIM))
            outs["sb_vs"].append(vf.reshape(bs, t, HEADS, HEAD_DIM))
            wo, bo = sb_wo[j].astype(BF16), zero_bias
        elif m == 1:
            wq, wkv, wexp = _mla_weights(mla_wuq[j], mla_wdkv[j], mla_wuk[j])
            wdq, wuv = mla_wdq[j].astype(BF16), mla_wuv[j].astype(BF16)
            q, ckv, kpe = _mla_proj(xp, norm_mix[i], wdq, mla_gq[j], wq, wkv, mla_gkv[j],
                                    _rope_tables(pos_p), s)
            k, v = _mla_expand(ckv, kpe, wexp, wuv)
            ap = _softmax_attn(q.reshape(bp, s, -1), k.reshape(bp, s, -1), v.reshape(bp, s, D),
                               tq=ROW_TILE, q_off=0)
            outs["mla_cp"].append(ckv.reshape(bp, s, MLA_KV_RANK))
            outs["mla_pp"].append(kpe[:, :MLA_ROPE].reshape(bp, s, MLA_ROPE))
            q, ckv, kpe = _mla_proj(xs, norm_mix[i], wdq, mla_gq[j], wq, wkv, mla_gkv[j],
                                    _rope_tables(pos_s), t)
            ckv_all = _with_cache(cache_mla_ckv[j], ckv.reshape(bs, t, MLA_KV_RANK), mla_keys)
            kpe_cache = jnp.pad(cache_mla_kpe[j], ((0, 0), (0, 0), (0, LANES - MLA_ROPE)))
            kpe_all = _with_cache(kpe_cache, kpe.reshape(bs, t, LANES), mla_keys)
            k, v = _mla_expand(ckv_all.reshape(bs * mla_keys, MLA_KV_RANK),
                               kpe_all.reshape(bs * mla_keys, LANES), wexp, wuv)
            as_ = _softmax_attn(q.reshape(bs, t, -1), k.reshape(bs, mla_keys, -1),
                                v.reshape(bs, mla_keys, D), tq=t, q_off=past // t)
            outs["mla_cs"].append(ckv.reshape(bs, t, MLA_KV_RANK))
            outs["mla_ps"].append(kpe[:, :MLA_ROPE].reshape(bs, t, MLA_ROPE))
            wo, bo = mla_wo[j].astype(BF16), zero_bias
        elif m == 2:
            w1 = conv_w1[j].astype(BF16)
            ap, st = _conv(xp.reshape(bp, s, D), norm_mix[i], w1, conv_b1[j],
                           jnp.zeros((bp, CONV_HIST, D), F32), conv_dw[j], conv_bdw[j],
                           conv_ln_g[j], conv_ln_b[j])
            outs["conv_p"].append(st)
            as_, st = _conv(xs.reshape(bs, t, D), norm_mix[i], w1, conv_b1[j], state_conv[j],
                            conv_dw[j], conv_bdw[j], conv_ln_g[j], conv_ln_b[j])
            outs["conv_s"].append(st)
            wo, bo = conv_w2[j].astype(BF16), conv_b2[j]
        else:
            w = band_wqkv[j].astype(BF16)
            bias_tiles = _band_bias_layouts(band_rel_bias[j])
            q, kf, vf, kb, vb = _qkv(xp, norm_mix[i], w, head_scale)
            ap = _band_attn(q.reshape(bp, s, D), kb.reshape(bp, s, D), vb.reshape(bp, s, D),
                            bias_tiles, q_off=0)
            keep = min(BAND_PAST, s)
            outs["band_kp"].append(kf.reshape(bp, s, HEADS, HEAD_DIM)[:, s - keep:])
            outs["band_vp"].append(vf.reshape(bp, s, HEADS, HEAD_DIM)[:, s - keep:])
            q, kf, vf, kb, vb = _qkv(xs, norm_mix[i], w, head_scale)
            n_past = cache_band_k.shape[2]
            local = n_past + BAND_TQ
            k_all = _with_cache(cache_band_k[j].reshape(bs, n_past, D), kb.reshape(bs, t, D), local)
            v_all = _with_cache(cache_band_v[j].reshape(bs, n_past, D), vb.reshape(bs, t, D), local)
            q_pad = jnp.pad(q.reshape(bs, t, D), ((0, 0), (0, BAND_TQ - t), (0, 0)))
            as_ = _band_attn(q_pad, k_all, v_all, bias_tiles, q_off=n_past // BAND_TQ)[:, :t]
            kk = jnp.concatenate([cache_band_k[j], kf.reshape(bs, t, HEADS, HEAD_DIM)], axis=1)
            vv = jnp.concatenate([cache_band_v[j], vf.reshape(bs, t, HEADS, HEAD_DIM)], axis=1)
            outs["band_ks"].append(kk[:, t:])
            outs["band_vs"].append(vv[:, t:])
            wo, bo = band_wo[j].astype(BF16), zero_bias

        w13, w2 = _ffn_weights(ffn_w1[i], ffn_w3[i], ffn_w2[i])
        final = i == depth - 1
        post = functools.partial(_post, wo=wo, bo=bo, gf=norm_ffn[i], w13=w13, w2=w2, gp=norm_ple[i],
                                 wg=ple_gate[i].astype(BF16), wp=ple_proj[i].astype(BF16),
                                 gfin=norm_final, final=final)
        xp = post(xp, ap.reshape(bp * s, D), p_prompt[i].reshape(bp * s, PLE_DIM))
        xs = post(xs, as_.reshape(bs * t, D), p_sample[i].reshape(bs * t, PLE_DIM))

    stack = lambda name: jnp.stack(outs[name])
    return (xp.reshape(bp, s, D), xs.reshape(bs, t, D),
            stack("sb_kp"), stack("sb_vp"), stack("sb_ks"), stack("sb_vs"),
            stack("mla_cp"), stack("mla_pp"), stack("mla_cs"), stack("mla_ps"),
            stack("conv_p"), stack("conv_s"),
            stack("band_kp"), stack("band_vp"), stack("band_ks"), stack("band_vs"))
```

```python
import functools
import math

import jax
import jax.numpy as jnp
from jax import lax
from jax.experimental import pallas as pl
from jax.experimental.pallas import tpu as pltpu

F32 = jnp.float32
BF16 = jnp.bfloat16

D = 1024
EPS = 1e-6
NEG = -1e30
HEADS = 16
HEAD_DIM = 64
CHUNK = 64
PLE_DIM = 256
FFN_HIDDEN = 2816
FFN_CHUNK = 256
N_FFN_CHUNKS = FFN_HIDDEN // FFN_CHUNK
MLA_Q_RANK = 256
MLA_KV_RANK = 128
MLA_NOPE = 64
MLA_ROPE = 32
MLA_SCALE = (MLA_NOPE + MLA_ROPE) ** -0.5
ROPE_THETA = 10000.0
CONV_WIDTH = 31
CONV_HIST = CONV_WIDTH - 1
HIST_PAD = 32
BAND_CHUNKS = 8
BAND_PAST = BAND_CHUNKS * CHUNK
REL_CLIP = 128
N_MIXERS = 4

LANES = 128
KEY_TILE = 256
SB_DEAD_BITS = 160.0
SM_KEY_TILE = 2 * KEY_TILE
SM_HEADS = 2
SM_BLOCKS = 2
LOG2E = math.log2(math.e)
ROW_TILE = 512
VMEM_LIMIT = 56 * 1024 * 1024


def _dot(a, b):
    return jnp.dot(a, b, preferred_element_type=F32)


def _dot_nt(a, b):
    return lax.dot_general(a, b, (((1,), (1,)), ((), ())), preferred_element_type=F32)


def _rms(x, g):
    return x * lax.rsqrt(jnp.mean(x * x, axis=-1, keepdims=True) + EPS) * g


def _sigmoid(x):
    return 1.0 / (1.0 + jnp.exp(-x))


def _params(sem):
    return pltpu.CompilerParams(dimension_semantics=sem, vmem_limit_bytes=VMEM_LIMIT)


def _resident(shape):
    zeros = (0,) * len(shape)
    return pl.BlockSpec(shape, lambda *_: zeros, pipeline_mode=pl.Buffered(1))


def _row_tile(rows):
    return min(ROW_TILE, rows)


def _chunk_of(idx):
    return lax.shift_right_arithmetic(idx, CHUNK.bit_length() - 1)


def _head_lanes(lane, h):
    return lane < HEAD_DIM if h == 0 else lane >= HEAD_DIM


def _rows_by_head(rows):
    lane = lax.broadcasted_iota(jnp.int32, rows.shape, 1)
    zero = jnp.zeros_like(rows)
    return jnp.concatenate([jnp.where(_head_lanes(lane, h), rows, zero) for h in range(2)], axis=0)


def _qkv_body(x_ref, g_ref, w_ref, q_ref, kf_ref, vf_ref, kb_ref, vb_ref, *, scale):
    h = _rms(x_ref[...], g_ref[...]).astype(BF16)
    q_ref[...] = (_dot(h, w_ref[:, 0:D]) * scale).astype(BF16)
    k = _dot(h, w_ref[:, D:2 * D])
    kf_ref[...] = k
    kb_ref[...] = k.astype(BF16)
    v = _dot(h, w_ref[:, 2 * D:3 * D])
    vf_ref[...] = v
    vb_ref[...] = v.astype(BF16)


def _qkv(x, g, w, scale):
    rows = x.shape[0]
    tm = _row_tile(rows)
    row = lambda i: (i, 0)
    out_sds = lambda dt: jax.ShapeDtypeStruct((rows, D), dt)
    return pl.pallas_call(
        functools.partial(_qkv_body, scale=scale),
        grid=(rows // tm,),
        in_specs=[pl.BlockSpec((tm, D), row), _resident((1, D)), _resident((D, 3 * D))],
        out_specs=[pl.BlockSpec((tm, D), row)] * 5,
        out_shape=[out_sds(BF16), out_sds(F32), out_sds(F32), out_sds(BF16), out_sds(BF16)],
        compiler_params=_params(("parallel",)),
        name="qkv_proj",
    )(x, g.reshape(1, D), w)


def _sb_attn_body(q_ref, k_ref, v_ref, tri_ref, o_ref, *, tq, q_off):
    q0 = (pl.program_id(2) + q_off) * tq
    first_diag = lax.div(q0, KEY_TILE)
    n_diag = max(1, tq // KEY_TILE)
    lane = lax.broadcasted_iota(jnp.int32, (tq, LANES), 1)
    diff = (lax.broadcasted_iota(jnp.int32, (tq, KEY_TILE), 1)
            - lax.broadcasted_iota(jnp.int32, (tq, KEY_TILE), 0))
    col_minus_row = jnp.concatenate([diff, diff], axis=1)
    q_pair = q_ref[0]
    tri = tri_ref[...]

    def tile(j, masked):
        k0 = pl.multiple_of(j * KEY_TILE, KEY_TILE)
        z = _dot_nt(q_pair, _rows_by_head(k_ref[0, pl.ds(k0, KEY_TILE), :]))
        neg_abs = pltpu.bitcast(pltpu.bitcast(z, jnp.uint32) | jnp.uint32(0x80000000), F32)
        drop = jnp.maximum(z, 0.0) + jnp.log(1.0 + jnp.exp2(neg_abs)) * LOG2E
        if masked:
            valid = col_minus_row < (q0 - k0)
            drop = jnp.where(valid, drop, 0.0)
        hi = drop.astype(BF16)
        later = jnp.concatenate([_dot(hi[:, :KEY_TILE], tri), _dot(hi[:, KEY_TILE:], tri)], axis=1)
        w = jnp.exp2((z - drop) - later)
        if masked:
            w = jnp.where(valid, w, 0.0)
        o = _dot(w.astype(BF16), _rows_by_head(v_ref[0, pl.ds(k0, KEY_TILE), :]))
        total = jnp.where(lane < HEAD_DIM, jnp.sum(drop[:, :KEY_TILE], axis=-1, keepdims=True),
                          jnp.sum(drop[:, KEY_TILE:], axis=-1, keepdims=True))
        return o, total

    def step(tiles, carry, masked):
        dropped, out = carry
        for j in tiles:
            o, total = tile(j, masked)
            out = out + jnp.exp2(-dropped) * o
            dropped = dropped + total
        return dropped, out

    def alive(carry):
        return (jnp.min(carry[0]) < SB_DEAD_BITS).astype(jnp.int32)

    def more(state):
        t, live, _ = state
        return jnp.logical_and(t < lax.div(first_diag, 2), live > 0)

    def farther(state):
        t, _, carry = state
        carry = step([first_diag - 1 - 2 * t, first_diag - 2 - 2 * t], carry, False)
        return t + 1, alive(carry), carry

    carry = (jnp.zeros((tq, LANES), F32), jnp.zeros((tq, LANES), F32))
    carry = step([first_diag + d for d in reversed(range(n_diag))], carry, True)
    carry = lax.while_loop(more, farther, (jnp.int32(0), alive(carry), carry))[2]
    o_ref[0] = carry[1].astype(BF16)


def _tri():
    j = lax.broadcasted_iota(jnp.int32, (KEY_TILE, KEY_TILE), 0)
    s = lax.broadcasted_iota(jnp.int32, (KEY_TILE, KEY_TILE), 1)
    return (j > s).astype(BF16)


def _sb_attn(q, k, v, *, tq, q_off):
    b, sq, _ = q.shape
    sk = k.shape[1]
    assert (q_off * tq) % (2 * KEY_TILE) == 0 and (tq % (2 * KEY_TILE) == 0 or sq == tq)
    assert (q_off + sq // tq - 1) * tq + max(tq, KEY_TILE) <= sk
    return pl.pallas_call(
        functools.partial(_sb_attn_body, tq=tq, q_off=q_off),
        grid=(b, HEADS // 2, sq // tq),
        in_specs=[pl.BlockSpec((1, tq, LANES), lambda bi, p, qi: (bi, qi, p)),
                  pl.BlockSpec((1, sk, LANES), lambda bi, p, qi: (bi, 0, p)),
                  pl.BlockSpec((1, sk, LANES), lambda bi, p, qi: (bi, 0, p)),
                  pl.BlockSpec((KEY_TILE, KEY_TILE), lambda bi, p, qi: (0, 0))],
        out_specs=pl.BlockSpec((1, tq, LANES), lambda bi, p, qi: (bi, qi, p)),
        out_shape=jax.ShapeDtypeStruct((b, sq, D), BF16),
        compiler_params=_params(("parallel", "parallel", "arbitrary")),
        name="sb_attn",
    )(q, k, v, _tri())


def _value_with_ones(v_tile, h):
    lane = lax.broadcasted_iota(jnp.int32, v_tile.shape, 1)
    return jnp.where(_head_lanes(lane, h), v_tile, jnp.ones_like(v_tile))


def _normalise_pair(accs):
    res = [a * pltpu.roll(1.0 / a, HEAD_DIM, 1) for a in accs]
    lane = lax.broadcasted_iota(jnp.int32, res[0].shape, 1)
    return jnp.where(lane < HEAD_DIM, res[0], res[1])


def _softmax_attn_body(q_ref, k_ref, v_ref, o_ref, *, tq, q_off, blocks):
    first = pl.program_id(2) * blocks + q_off
    chunk_diff = (_chunk_of(lax.broadcasted_iota(jnp.int32, (tq, SM_KEY_TILE), 1))
                  - _chunk_of(lax.broadcasted_iota(jnp.int32, (tq, SM_KEY_TILE), 0)))

    def chain(blk, h, j, state, masked):
        m, acc = state
        k0 = pl.multiple_of(j * SM_KEY_TILE, SM_KEY_TILE)
        s = _dot_nt(q_ref[0, tq * blk:tq * (blk + 1), LANES * h:LANES * (h + 1)],
                    k_ref[0, pl.ds(k0, SM_KEY_TILE), LANES * h:LANES * (h + 1)])
        if masked:
            s = jnp.where(chunk_diff <= _chunk_of((first + blk) * tq - k0), s, NEG)
        m_new = jnp.maximum(m, jnp.max(s, axis=-1, keepdims=True))
        p = jnp.exp2(s - m_new)
        v_pair = v_ref[0, pl.ds(k0, SM_KEY_TILE), LANES * (h // 2):LANES * (h // 2 + 1)]
        return m_new, acc * jnp.exp2(m - m_new) + _dot(p.astype(BF16), _value_with_ones(v_pair, h % 2))

    def step(work, carry):
        new = [list(c) for c in carry]
        for blk, j, masked in work:
            for h in range(SM_HEADS):
                new[blk][h] = chain(blk, h, j, new[blk][h], masked)
        return tuple(tuple(c) for c in new)

    carry = tuple(tuple((jnp.full((tq, 1), NEG, F32), jnp.zeros((tq, LANES), F32)) for _ in range(SM_HEADS))
                  for _ in range(blocks))
    if blocks == 1:
        last = lax.div(first * tq, SM_KEY_TILE)
        pairs = lax.div(last, 2)
        carry = lax.fori_loop(0, pairs, lambda t, c: step([(0, 2 * t, False), (0, 2 * t + 1, False)], c), carry)
        carry = lax.fori_loop(2 * pairs, last, lambda t, c: step([(0, t, False)], c), carry)
        carry = step([(0, last, True)], carry)
    else:
        carry = lax.fori_loop(
            0, lax.div(first, 2),
            lambda t, c: step([(blk, 2 * t + u, False) for u in range(2) for blk in range(blocks)], c), carry)
        carry = step([(blk, first + i, blk == i) for i in range(blocks) for blk in range(i, blocks)], carry)
    for blk in range(blocks):
        for pair in range(SM_HEADS // 2):
            o_ref[0, tq * blk:tq * (blk + 1), LANES * pair:LANES * (pair + 1)] = _normalise_pair(
                [carry[blk][2 * pair][1], carry[blk][2 * pair + 1][1]]).astype(BF16)


def _softmax_attn(q, k, v, *, tq, q_off):
    b, sq, _ = q.shape
    sk = k.shape[1]
    assert tq <= SM_KEY_TILE and SM_KEY_TILE % tq == 0 and sk % SM_KEY_TILE == 0
    paired = tq == SM_KEY_TILE and (sq // tq) % SM_BLOCKS == 0 and q_off % 2 == 0
    blocks = SM_BLOCKS if paired else 1
    qk_lanes, v_lanes = SM_HEADS * LANES, SM_HEADS * HEAD_DIM
    return pl.pallas_call(
        functools.partial(_softmax_attn_body, tq=tq, q_off=q_off, blocks=blocks),
        grid=(b, HEADS // SM_HEADS, sq // (tq * blocks)),
        in_specs=[pl.BlockSpec((1, tq * blocks, qk_lanes), lambda bi, p, qi: (bi, qi, p)),
                  pl.BlockSpec((1, sk, qk_lanes), lambda bi, p, qi: (bi, 0, p)),
                  pl.BlockSpec((1, sk, v_lanes), lambda bi, p, qi: (bi, 0, p))],
        out_specs=pl.BlockSpec((1, tq * blocks, v_lanes), lambda bi, p, qi: (bi, qi, p)),
        out_shape=jax.ShapeDtypeStruct((b, sq, D), BF16),
        compiler_params=_params(("parallel", "parallel", "arbitrary")),
        name="mla_attn",
    )(q, k, v)


BAND_TQ = 4 * CHUNK


BAND_KEYS = 3 * KEY_TILE
BAND_BLOCKS = 4


def _band_attn_body(q_ref, k_ref, v_ref, bias_ref, o_ref, *, q_off, blocks):
    row = lax.broadcasted_iota(jnp.int32, (2 * BAND_KEYS, LANES), 0)
    lane = lax.broadcasted_iota(jnp.int32, (2 * BAND_KEYS, LANES), 1)
    ones = jnp.where((row < BAND_KEYS) == (lane < HEAD_DIM), 1.0, 0.0).astype(BF16)
    for g in range(blocks):
        rows = slice(BAND_TQ * g, BAND_TQ * (g + 1))
        qi = pl.program_id(2) * blocks + g + q_off
        k0 = pl.multiple_of(jnp.maximum(qi - 2, 0) * KEY_TILE, KEY_TILE)
        k_win = k_ref[0, pl.ds(k0, BAND_KEYS), :]
        v_win = v_ref[0, pl.ds(k0, BAND_KEYS), :]
        s = _dot_nt(q_ref[0, rows, :], _rows_by_head(k_win)) + bias_ref[0, jnp.minimum(qi, 2)]
        p = jnp.concatenate(
            [jnp.exp2(sh - jnp.max(sh, axis=-1, keepdims=True)) for sh in (s[:, :BAND_KEYS], s[:, BAND_KEYS:])],
            axis=1).astype(BF16)
        acc = _dot(p, jnp.concatenate([_rows_by_head(v_win), ones], axis=1))
        o_ref[0, rows, :] = (acc[:, :LANES] * (1.0 / acc[:, LANES:])).astype(BF16)


def _band_bias_tiles(rel_bias):
    table = rel_bias.astype(F32) * LOG2E
    heads = table.shape[0]
    n = BAND_TQ + KEY_TILE - 1
    tiles = []
    for t in range(3):
        d_lo = BAND_PAST - KEY_TILE * t - (KEY_TILE - 1)
        below = max(0, min(n, -REL_CLIP - d_lo))
        above = max(0, min(n, d_lo + n - 1 - REL_CLIP))
        mid_lo = d_lo + below + REL_CLIP
        diag = jnp.concatenate([jnp.broadcast_to(table[:, :1], (heads, below)),
                                table[:, mid_lo:mid_lo + n - below - above],
                                jnp.broadcast_to(table[:, -1:], (heads, above))], axis=1)
        rows = jnp.tile(diag, (1, BAND_TQ + 1))[:, :BAND_TQ * (n + 1)].reshape(heads, BAND_TQ, n + 1)
        tiles.append(rows[:, :, KEY_TILE - 1::-1])
    return jnp.stack(tiles, axis=1)


def _band_bias_layouts(rel_bias):
    tiles = _band_bias_tiles(rel_bias)
    heads = tiles.shape[0]
    row_chunk = (jnp.arange(BAND_TQ, dtype=jnp.int32) // CHUNK)[:, None]
    col_chunk = (jnp.arange(KEY_TILE, dtype=jnp.int32) // CHUNK)[None, :]
    eight_back = jnp.where(row_chunk <= col_chunk, tiles[:, 0], NEG)
    four_back = tiles[:, 1]
    own = jnp.where(col_chunk <= row_chunk, tiles[:, 2], NEG)
    ahead = jnp.full_like(own, NEG)
    layouts = [jnp.concatenate(slots, axis=-1) for slots in
               ((own, ahead, ahead), (four_back, own, ahead), (eight_back, four_back, own))]
    per_head = jnp.stack(layouts, axis=1)
    pairs = per_head.reshape(heads // 2, 2, 3, BAND_TQ, BAND_KEYS)
    return jnp.concatenate([pairs[:, 0], pairs[:, 1]], axis=-1)


def _band_attn(q, k, v, bias_layouts, *, q_off):
    b, sq, _ = q.shape
    sk = k.shape[1]
    assert sk >= BAND_KEYS and (q_off + sq // BAND_TQ) * BAND_TQ <= sk
    blocks = math.gcd(sq // BAND_TQ, BAND_BLOCKS)
    rows = blocks * BAND_TQ
    return pl.pallas_call(
        functools.partial(_band_attn_body, q_off=q_off, blocks=blocks),
        grid=(HEADS // 2, b, sq // rows),
        in_specs=[pl.BlockSpec((1, rows, LANES), lambda p, bi, qi: (bi, qi, p)),
                  pl.BlockSpec((1, sk, LANES), lambda p, bi, qi: (bi, 0, p)),
                  pl.BlockSpec((1, sk, LANES), lambda p, bi, qi: (bi, 0, p)),
                  pl.BlockSpec((1, 3, BAND_TQ, 2 * BAND_KEYS), lambda p, bi, qi: (p, 0, 0, 0))],
        out_specs=pl.BlockSpec((1, rows, LANES), lambda p, bi, qi: (bi, qi, p)),
        out_shape=jax.ShapeDtypeStruct((b, sq, D), BF16),
        compiler_params=_params(("parallel", "parallel", "arbitrary")),
        name="band_attn",
    )(q, k, v, bias_layouts)


def _mla_proj_body(x_ref, g_ref, wdq_ref, gq_ref, wq_ref, wkv_ref, gkv_ref,
                   qcos_ref, qsin_ref, kcos_ref, ksin_ref, q_ref, ckv_ref, kpe_ref):
    h = _rms(x_ref[...], g_ref[...]).astype(BF16)
    cq = _rms(_dot(h, wdq_ref[...]), gq_ref[...]).astype(BF16)
    qq = _dot(cq, wq_ref[...])
    qcos, qsin = qcos_ref[...], qsin_ref[...]
    half = HEADS * LANES
    for hd in range(HEADS):
        lo = LANES * hd
        q_ref[:, lo:lo + LANES] = (qq[:, lo:lo + LANES] * qcos
                                   + qq[:, half + lo:half + lo + LANES] * qsin).astype(BF16)
    kv = _dot(h, wkv_ref[...])
    ckv_ref[...] = _rms(kv[:, 0:LANES], gkv_ref[...])
    kpe_ref[...] = kv[:, LANES:2 * LANES] * kcos_ref[...] + kv[:, 2 * LANES:3 * LANES] * ksin_ref[...]


def _rope_tables(pos):
    half = MLA_ROPE // 2
    inv = ROPE_THETA ** (-jnp.arange(half, dtype=F32) / half)
    ang = pos.astype(F32)[:, None] * inv[None, :]
    cos = jnp.concatenate([jnp.cos(ang)] * 2, axis=-1)
    sin = jnp.concatenate([jnp.sin(ang)] * 2, axis=-1)
    n = pos.shape[0]
    pad = jnp.zeros((n, LANES - MLA_NOPE - MLA_ROPE), F32)
    q_scale = MLA_SCALE * LOG2E
    qcos = jnp.concatenate([jnp.ones((n, MLA_NOPE), F32), cos, pad], axis=-1) * q_scale
    qsin = jnp.concatenate([jnp.zeros((n, MLA_NOPE), F32), sin, pad], axis=-1) * q_scale
    kpad = jnp.zeros((n, LANES - MLA_ROPE), F32)
    kcos = jnp.concatenate([cos, kpad], axis=-1)
    ksin = jnp.concatenate([sin, kpad], axis=-1)
    return qcos, qsin, kcos, ksin


def _rotary_partner(w):
    half = MLA_ROPE // 2
    return jnp.concatenate([-w[..., half:], w[..., :half]], axis=-1)


def _mla_weights(wuq, wdkv, wuk):
    r = wuq.shape[0]
    wq3 = wuq.reshape(r, HEADS, MLA_NOPE + MLA_ROPE)
    zpad = jnp.zeros((r, HEADS, LANES - MLA_NOPE - MLA_ROPE), wuq.dtype)
    plain = jnp.concatenate([wq3, zpad], axis=-1).reshape(r, HEADS * LANES)
    partner = jnp.concatenate([jnp.zeros((r, HEADS, MLA_NOPE), wuq.dtype),
                               _rotary_partner(wq3[..., MLA_NOPE:]), zpad], axis=-1)
    wq = jnp.concatenate([plain, partner.reshape(r, HEADS * LANES)], axis=-1).astype(BF16)
    rope_w = wdkv[:, MLA_KV_RANK:]
    kz = jnp.zeros((D, LANES - MLA_ROPE), wdkv.dtype)
    wkv = jnp.concatenate([wdkv[:, :MLA_KV_RANK], rope_w, kz, _rotary_partner(rope_w), kz],
                          axis=-1).astype(BF16)
    top = jnp.concatenate([wuk.reshape(MLA_KV_RANK, HEADS, MLA_NOPE),
                           jnp.zeros((MLA_KV_RANK, HEADS, LANES - MLA_NOPE), wuk.dtype)], axis=-1)
    place = jnp.concatenate([jnp.zeros((MLA_ROPE, MLA_NOPE), F32), jnp.eye(MLA_ROPE, dtype=F32),
                             jnp.zeros((MLA_ROPE, LANES - MLA_NOPE - MLA_ROPE), F32)], axis=-1)
    place = jnp.concatenate([place, jnp.zeros((LANES - MLA_ROPE, LANES), F32)], axis=0)
    bottom = jnp.broadcast_to(place[:, None, :], (LANES, HEADS, LANES))
    wexp = jnp.concatenate([top, bottom], axis=0).reshape(2 * LANES, HEADS * LANES).astype(BF16)
    return wq, wkv, wexp


def _mla_proj(x, g, wdq, gq, wq, wkv, gkv, tables, seq):
    rows = x.shape[0]
    tm = _row_tile(rows)
    if seq >= tm:
        n_pos_tiles = seq // tm
        pos_map = lambda i: (i % n_pos_tiles, 0)
    else:
        tables = [jnp.tile(t, (tm // seq, 1)) for t in tables]
        pos_map = lambda i: (0, 0)
    row = lambda i: (i, 0)
    return pl.pallas_call(
        _mla_proj_body,
        grid=(rows // tm,),
        in_specs=[pl.BlockSpec((tm, D), row), _resident((1, D)), _resident((D, MLA_Q_RANK)),
                  _resident((1, MLA_Q_RANK)), _resident((MLA_Q_RANK, 2 * HEADS * LANES)),
                  _resident((D, 3 * LANES)), _resident((1, MLA_KV_RANK))]
                 + [pl.BlockSpec((tm, LANES), pos_map)] * 4,
        out_specs=[pl.BlockSpec((tm, HEADS * LANES), row), pl.BlockSpec((tm, LANES), row),
                   pl.BlockSpec((tm, LANES), row)],
        out_shape=[jax.ShapeDtypeStruct((rows, HEADS * LANES), BF16),
                   jax.ShapeDtypeStruct((rows, MLA_KV_RANK), F32),
                   jax.ShapeDtypeStruct((rows, LANES), F32)],
        compiler_params=_params(("parallel",)),
        name="mla_proj",
    )(x, g.reshape(1, D), wdq, gq.reshape(1, MLA_Q_RANK), wq, wkv, gkv.reshape(1, MLA_KV_RANK), *tables)


def _mla_expand_body(ckv_ref, kpe_ref, wexp_ref, wuv_ref, k_ref, v_ref):
    c = ckv_ref[...].astype(BF16)
    cat = jnp.concatenate([c, kpe_ref[...].astype(BF16)], axis=-1)
    k_ref[...] = _dot(cat, wexp_ref[...]).astype(BF16)
    v_ref[...] = _dot(c, wuv_ref[...]).astype(BF16)


def _mla_expand(ckv, kpe, wexp, wuv):
    rows = ckv.shape[0]
    tm = _row_tile(rows)
    row = lambda i: (i, 0)
    return pl.pallas_call(
        _mla_expand_body,
        grid=(rows // tm,),
        in_specs=[pl.BlockSpec((tm, LANES), row), pl.BlockSpec((tm, LANES), row),
                  _resident((2 * LANES, HEADS * LANES)), _resident((MLA_KV_RANK, D))],
        out_specs=[pl.BlockSpec((tm, HEADS * LANES), row), pl.BlockSpec((tm, D), row)],
        out_shape=[jax.ShapeDtypeStruct((rows, HEADS * LANES), BF16),
                   jax.ShapeDtypeStruct((rows, D), BF16)],
        compiler_params=_params(("parallel",)),
        name="mla_expand",
    )(ckv, kpe, wexp, wuv)


CONV_ROWS = 32
LANE_COLS = D // LANES


def _conv_body(x_ref, g_ref, w1_ref, b1_ref, hist_ref, dw_ref, bdw_ref, lng_ref, lnb_ref,
               a_ref, st_ref, gbuf, ybuf, *, tm):
    @pl.when(pl.program_id(1) == 0)
    def _():
        for c in range(LANE_COLS):
            gbuf[c, 0:HIST_PAD, :] = hist_ref[0, :, LANES * c:LANES * (c + 1)]

    h = _rms(x_ref[0], g_ref[...]).astype(BF16)
    u = _dot(h, w1_ref[...]) + b1_ref[...]
    glu = u[:, :D] * _sigmoid(u[:, D:])
    for c in range(LANE_COLS):
        gbuf[c, HIST_PAD:HIST_PAD + tm, :] = glu[:, LANES * c:LANES * (c + 1)]

    first = HIST_PAD - CONV_HIST

    def rows(i, _):
        r0 = pl.multiple_of(i * CONV_ROWS, CONV_ROWS)
        for c in range(LANE_COLS):
            cols = slice(LANES * c, LANES * (c + 1))
            acc = jnp.broadcast_to(bdw_ref[:, cols], (CONV_ROWS, LANES))
            for w in range(CONV_WIDTH):
                acc = acc + dw_ref[w:w + 1, cols] * gbuf[c, pl.ds(r0 + first + w, CONV_ROWS), :]
            ybuf[pl.ds(r0, CONV_ROWS), cols] = acc
        return 0

    lax.fori_loop(0, tm // CONV_ROWS, rows, 0)

    y = ybuf[...]
    yc = y - jnp.mean(y, axis=-1, keepdims=True)
    var = jnp.mean(yc * yc, axis=-1, keepdims=True)
    yn = yc * lax.rsqrt(var + EPS) * lng_ref[...] + lnb_ref[...]
    a_ref[0] = (yn * _sigmoid(yn)).astype(BF16)

    for c in range(LANE_COLS):
        tail = gbuf[c, tm:tm + HIST_PAD, :]
        st_ref[0, :, LANES * c:LANES * (c + 1)] = tail
        gbuf[c, 0:HIST_PAD, :] = tail


def _conv(x, g, w1, b1, hist, dw, bdw, lng, lnb):
    b, s, _ = x.shape
    tm = _row_tile(s)
    hist = jnp.pad(hist.astype(F32), ((0, 0), (HIST_PAD - CONV_HIST, 0), (0, 0)))
    dw = jnp.pad(dw.astype(F32), ((0, HIST_PAD - CONV_WIDTH), (0, 0)))
    a, st = pl.pallas_call(
        functools.partial(_conv_body, tm=tm),
        grid=(b, s // tm),
        in_specs=[pl.BlockSpec((1, tm, D), lambda bi, t: (bi, t, 0)), _resident((1, D)),
                  _resident((D, 2 * D)), _resident((1, 2 * D)),
                  pl.BlockSpec((1, HIST_PAD, D), lambda bi, t: (bi, 0, 0)),
                  _resident((HIST_PAD, D)), _resident((1, D)), _resident((1, D)), _resident((1, D))],
        out_specs=[pl.BlockSpec((1, tm, D), lambda bi, t: (bi, t, 0)),
                   pl.BlockSpec((1, HIST_PAD, D), lambda bi, t: (bi, 0, 0))],
        out_shape=[jax.ShapeDtypeStruct((b, s, D), BF16),
                   jax.ShapeDtypeStruct((b, HIST_PAD, D), F32)],
        scratch_shapes=[pltpu.VMEM((LANE_COLS, HIST_PAD + tm, LANES), F32), pltpu.VMEM((tm, D), F32)],
        compiler_params=_params(("parallel", "arbitrary")),
        name="conv_module",
    )(x, g.reshape(1, D), w1, b1.reshape(1, 2 * D), hist, dw, bdw.reshape(1, D),
      lng.reshape(1, D), lnb.reshape(1, D))
    return a, st[:, HIST_PAD - CONV_HIST:]


def _post_body(x_ref, a_ref, p_ref, wo_ref, bo_ref, gf_ref, w13_ref, w2_ref, gp_ref, wg_ref, wp_ref,
               gfin_ref, o_ref, h_s, acc_s, *, final):
    x1 = x_ref[...] + _dot(a_ref[...], wo_ref[...]) + bo_ref[...]
    h_s[...] = _rms(x1, gf_ref[...]).astype(BF16)
    acc_s[...] = x1

    def gated(c):
        uv = _dot(h_s[...], w13_ref[c])
        u = uv[:, :FFN_CHUNK]
        return _dot((u * _sigmoid(u) * uv[:, FFN_CHUNK:]).astype(BF16), w2_ref[c])

    def chunk_pair(i, _):
        acc_s[...] += gated(2 * i) + gated(2 * i + 1)
        return 0

    lax.fori_loop(0, N_FFN_CHUNKS // 2, chunk_pair, 0)
    for c in range(N_FFN_CHUNKS // 2 * 2, N_FFN_CHUNKS):
        acc_s[...] += gated(c)
    x2 = acc_s[...]
    gate = _sigmoid(_dot(_rms(x2, gp_ref[...]).astype(BF16), wg_ref[...]))
    x3 = x2 + gate * _dot(p_ref[...].astype(BF16), wp_ref[...])
    o_ref[...] = _rms(x3, gfin_ref[...]) if final else x3


def _post(x, a, p, wo, bo, gf, w13, w2, gp, wg, wp, gfin, *, final):
    rows = x.shape[0]
    tm = _row_tile(rows)
    row = lambda i: (i, 0)
    return pl.pallas_call(
        functools.partial(_post_body, final=final),
        grid=(rows // tm,),
        in_specs=[pl.BlockSpec((tm, D), row), pl.BlockSpec((tm, D), row), pl.BlockSpec((tm, PLE_DIM), row),
                  _resident((D, D)), _resident((1, D)), _resident((1, D)),
                  _resident((N_FFN_CHUNKS, D, 2 * FFN_CHUNK)), _resident((N_FFN_CHUNKS, FFN_CHUNK, D)),
                  _resident((1, D)), _resident((D, D)), _resident((PLE_DIM, D)), _resident((1, D))],
        out_specs=pl.BlockSpec((tm, D), row),
        out_shape=jax.ShapeDtypeStruct((rows, D), F32),
        scratch_shapes=[pltpu.VMEM((tm, D), BF16), pltpu.VMEM((tm, D), F32)],
        compiler_params=_params(("parallel",)),
        name="post_mixer",
    )(x, a, p, wo, bo.reshape(1, D), gf.reshape(1, D), w13, w2, gp.reshape(1, D), wg, wp,
      gfin.reshape(1, D))


def _ffn_weights(w1, w3, w2):
    split = lambda w: w.reshape(D, N_FFN_CHUNKS, FFN_CHUNK).transpose(1, 0, 2)
    w13 = jnp.concatenate([split(w1), split(w3)], axis=-1).astype(BF16)
    return w13, w2.reshape(N_FFN_CHUNKS, FFN_CHUNK, D).astype(BF16)


def _with_cache(cache, new, total):
    b, past, w = cache.shape
    pad = jnp.zeros((b, total - past - new.shape[1], w), new.dtype)
    return jnp.concatenate([cache.astype(new.dtype), new, pad], axis=1)


def kernel(x_prompt, x_sample, p_prompt, p_sample, cache_sb_k, cache_sb_v, cache_mla_ckv, cache_mla_kpe, state_conv, cache_band_k, cache_band_v, norm_mix, norm_ffn, norm_ple, norm_final, ffn_w1, ffn_w3, ffn_w2, ple_proj, ple_gate, sb_wqkv, sb_wo, mla_wdq, mla_gq, mla_wuq, mla_wdkv, mla_gkv, mla_wuk, mla_wuv, mla_wo, conv_w1, conv_b1, conv_dw, conv_bdw, conv_ln_g, conv_ln_b, conv_w2, conv_b2, band_wqkv, band_rel_bias, band_wo):
    bp, s, _ = x_prompt.shape
    bs, t, _ = x_sample.shape
    depth = norm_mix.shape[0]
    past = cache_sb_k.shape[2]
    xp = x_prompt.reshape(bp * s, D)
    xs = x_sample.reshape(bs * t, D)
    pos_p = jnp.arange(s, dtype=jnp.int32)
    pos_s = past + jnp.arange(t, dtype=jnp.int32)
    zero_bias = jnp.zeros((D,), F32)
    head_scale = HEAD_DIM ** -0.5 * LOG2E
    sample_keys = -(-(past + t) // KEY_TILE) * KEY_TILE
    mla_keys = -(-(past + t) // SM_KEY_TILE) * SM_KEY_TILE

    outs = {name: [] for name in ("sb_kp", "sb_vp", "sb_ks", "sb_vs", "mla_cp", "mla_pp", "mla_cs", "mla_ps",
                                  "conv_p", "conv_s", "band_kp", "band_vp", "band_ks", "band_vs")}

    for i in range(depth):
        m, j = i % N_MIXERS, i // N_MIXERS
        if m == 0:
            w = sb_wqkv[j].astype(BF16)
            q, kf, vf, kb, vb = _qkv(xp, norm_mix[i], w, head_scale)
            ap = _sb_attn(q.reshape(bp, s, D), kb.reshape(bp, s, D), vb.reshape(bp, s, D),
                          tq=ROW_TILE, q_off=0)
            outs["sb_kp"].append(kf.reshape(bp, s, HEADS, HEAD_DIM))
            outs["sb_vp"].append(vf.reshape(bp, s, HEADS, HEAD_DIM))
            q, kf, vf, kb, vb = _qkv(xs, norm_mix[i], w, head_scale)
            k_all = _with_cache(cache_sb_k[j].reshape(bs, past, D), kb.reshape(bs, t, D), sample_keys)
            v_all = _with_cache(cache_sb_v[j].reshape(bs, past, D), vb.reshape(bs, t, D), sample_keys)
            as_ = _sb_attn(q.reshape(bs, t, D), k_all, v_all, tq=t, q_off=past // t)
            outs["sb_ks"].append(kf.reshape(bs, t, HEADS, HEAD_DIM))
            outs["sb_vs"].append(vf.reshape(bs, t, HEADS, HEAD_DIM))
            wo, bo = sb_wo[j].astype(BF16), zero_bias
        elif m == 1:
            wq, wkv, wexp = _mla_weights(mla_wuq[j], mla_wdkv[j], mla_wuk[j])
            wdq, wuv = mla_wdq[j].astype(BF16), mla_wuv[j].astype(BF16)
            q, ckv, kpe = _mla_proj(xp, norm_mix[i], wdq, mla_gq[j], wq, wkv, mla_gkv[j],
                                    _rope_tables(pos_p), s)
            k, v = _mla_expand(ckv, kpe, wexp, wuv)
            ap = _softmax_attn(q.reshape(bp, s, -1), k.reshape(bp, s, -1), v.reshape(bp, s, D),
                               tq=ROW_TILE, q_off=0)
            outs["mla_cp"].append(ckv.reshape(bp, s, MLA_KV_RANK))
            outs["mla_pp"].append(kpe[:, :MLA_ROPE].reshape(bp, s, MLA_ROPE))
            q, ckv, kpe = _mla_proj(xs, norm_mix[i], wdq, mla_gq[j], wq, wkv, mla_gkv[j],
                                    _rope_tables(pos_s), t)
            ckv_all = _with_cache(cache_mla_ckv[j], ckv.reshape(bs, t, MLA_KV_RANK), mla_keys)
            kpe_cache = jnp.pad(cache_mla_kpe[j], ((0, 0), (0, 0), (0, LANES - MLA_ROPE)))
            kpe_all = _with_cache(kpe_cache, kpe.reshape(bs, t, LANES), mla_keys)
            k, v = _mla_expand(ckv_all.reshape(bs * mla_keys, MLA_KV_RANK),
                               kpe_all.reshape(bs * mla_keys, LANES), wexp, wuv)
            as_ = _softmax_attn(q.reshape(bs, t, -1), k.reshape(bs, mla_keys, -1),
                                v.reshape(bs, mla_keys, D), tq=t, q_off=past // t)
            outs["mla_cs"].append(ckv.reshape(bs, t, MLA_KV_RANK))
            outs["mla_ps"].append(kpe[:, :MLA_ROPE].reshape(bs, t, MLA_ROPE))
            wo, bo = mla_wo[j].astype(BF16), zero_bias
        elif m == 2:
            w1 = conv_w1[j].astype(BF16)
            ap, st = _conv(xp.reshape(bp, s, D), norm_mix[i], w1, conv_b1[j],
                           jnp.zeros((bp, CONV_HIST, D), F32), conv_dw[j], conv_bdw[j],
                           conv_ln_g[j], conv_ln_b[j])
            outs["conv_p"].append(st)
            as_, st = _conv(xs.reshape(bs, t, D), norm_mix[i], w1, conv_b1[j], state_conv[j],
                            conv_dw[j], conv_bdw[j], conv_ln_g[j], conv_ln_b[j])
            outs["conv_s"].append(st)
            wo, bo = conv_w2[j].astype(BF16), conv_b2[j]
        else:
            w = band_wqkv[j].astype(BF16)
            bias_tiles = _band_bias_layouts(band_rel_bias[j])
            q, kf, vf, kb, vb = _qkv(xp, norm_mix[i], w, head_scale)
            ap = _band_attn(q.reshape(bp, s, D), kb.reshape(bp, s, D), vb.reshape(bp, s, D),
                            bias_tiles, q_off=0)
            keep = min(BAND_PAST, s)
            outs["band_kp"].append(kf.reshape(bp, s, HEADS, HEAD_DIM)[:, s - keep:])
            outs["band_vp"].append(vf.reshape(bp, s, HEADS, HEAD_DIM)[:, s - keep:])
            q, kf, vf, kb, vb = _qkv(xs, norm_mix[i], w, head_scale)
            n_past = cache_band_k.shape[2]
            local = n_past + BAND_TQ
            k_all = _with_cache(cache_band_k[j].reshape(bs, n_past, D), kb.reshape(bs, t, D), local)
            v_all = _with_cache(cache_band_v[j].reshape(bs, n_past, D), vb.reshape(bs, t, D), local)
            q_pad = jnp.pad(q.reshape(bs, t, D), ((0, 0), (0, BAND_TQ - t), (0, 0)))
            as_ = _band_attn(q_pad, k_all, v_all, bias_tiles, q_off=n_past // BAND_TQ)[:, :t]
            kk = jnp.concatenate([cache_band_k[j], kf.reshape(bs, t, HEADS, HEAD_DIM)], axis=1)
            vv = jnp.concatenate([cache_band_v[j], vf.reshape(bs, t, HEADS, HEAD_DIM)], axis=1)
            outs["band_ks"].append(kk[:, t:])
            outs["band_vs"].append(vv[:, t:])
            wo, bo = band_wo[j].astype(BF16), zero_bias

        w13, w2 = _ffn_weights(ffn_w1[i], ffn_w3[i], ffn_w2[i])
        final = i == depth - 1
        post = functools.partial(_post, wo=wo, bo=bo, gf=norm_ffn[i], w13=w13, w2=w2, gp=norm_ple[i],
                                 wg=ple_gate[i].astype(BF16), wp=ple_proj[i].astype(BF16),
                                 gfin=norm_final, final=final)
        xp = post(xp, ap.reshape(bp * s, D), p_prompt[i].reshape(bp * s, PLE_DIM))
        xs = post(xs, as_.reshape(bs * t, D), p_sample[i].reshape(bs * t, PLE_DIM))

    stack = lambda name: jnp.stack(outs[name])
    return (xp.reshape(bp, s, D), xs.reshape(bs, t, D),
            stack("sb_kp"), stack("sb_vp"), stack("sb_ks"), stack("sb_vs"),
            stack("mla_cp"), stack("mla_pp"), stack("mla_cs"), stack("mla_ps"),
            stack("conv_p"), stack("conv_s"),
            stack("band_kp"), stack("band_vp"), stack("band_ks"), stack("band_vs"))
```

```python
import functools
import math

import jax
import jax.numpy as jnp
from jax import lax
from jax.experimental import pallas as pl
from jax.experimental.pallas import tpu as pltpu

F32 = jnp.float32
BF16 = jnp.bfloat16

D = 1024
EPS = 1e-6
NEG = -1e30
HEADS = 16
HEAD_DIM = 64
CHUNK = 64
PLE_DIM = 256
FFN_HIDDEN = 2816
FFN_CHUNK = 256
N_FFN_CHUNKS = FFN_HIDDEN // FFN_CHUNK
MLA_Q_RANK = 256
MLA_KV_RANK = 128
MLA_NOPE = 64
MLA_ROPE = 32
MLA_SCALE = (MLA_NOPE + MLA_ROPE) ** -0.5
ROPE_THETA = 10000.0
CONV_WIDTH = 31
CONV_HIST = CONV_WIDTH - 1
HIST_PAD = 32
BAND_CHUNKS = 8
BAND_PAST = BAND_CHUNKS * CHUNK
REL_CLIP = 128
N_MIXERS = 4

LANES = 128
KEY_TILE = 256
SB_DEAD_BITS = 160.0
SM_KEY_TILE = 2 * KEY_TILE
SM_HEADS = 2
SM_BLOCKS = 2
LOG2E = math.log2(math.e)
ROW_TILE = 512
VMEM_LIMIT = 56 * 1024 * 1024


def _dot(a, b):
    return jnp.dot(a, b, preferred_element_type=F32)


def _dot_nt(a, b):
    return lax.dot_general(a, b, (((1,), (1,)), ((), ())), preferred_element_type=F32)


def _rms(x, g):
    return x * lax.rsqrt(jnp.mean(x * x, axis=-1, keepdims=True) + EPS) * g


def _sigmoid(x):
    return 1.0 / (1.0 + jnp.exp(-x))


def _params(sem):
    return pltpu.CompilerParams(dimension_semantics=sem, vmem_limit_bytes=VMEM_LIMIT)


def _resident(shape):
    zeros = (0,) * len(shape)
    return pl.BlockSpec(shape, lambda *_: zeros, pipeline_mode=pl.Buffered(1))


def _row_tile(rows):
    return min(ROW_TILE, rows)


def _chunk_of(idx):
    return lax.shift_right_arithmetic(idx, CHUNK.bit_length() - 1)


def _head_lanes(lane, h):
    return lane < HEAD_DIM if h == 0 else lane >= HEAD_DIM


def _rows_by_head(rows):
    lane = lax.broadcasted_iota(jnp.int32, rows.shape, 1)
    zero = jnp.zeros_like(rows)
    return jnp.concatenate([jnp.where(_head_lanes(lane, h), rows, zero) for h in range(2)], axis=0)


def _qkv_body(x_ref, g_ref, w_ref, q_ref, kf_ref, vf_ref, kb_ref, vb_ref, *, scale):
    h = _rms(x_ref[...], g_ref[...]).astype(BF16)
    q_ref[...] = (_dot(h, w_ref[:, 0:D]) * scale).astype(BF16)
    k = _dot(h, w_ref[:, D:2 * D])
    kf_ref[...] = k.reshape(kf_ref.shape)
    kb_ref[...] = k.astype(BF16)
    v = _dot(h, w_ref[:, 2 * D:3 * D])
    vf_ref[...] = v.reshape(vf_ref.shape)
    vb_ref[...] = v.astype(BF16)


def _qkv(x, g, w, scale):
    rows = x.shape[0]
    tm = _row_tile(rows)
    row = lambda i: (i, 0)
    out_sds = lambda dt: jax.ShapeDtypeStruct((rows, D), dt)
    heads_spec = pl.BlockSpec((tm, HEADS, HEAD_DIM), lambda i: (i, 0, 0))
    heads_sds = jax.ShapeDtypeStruct((rows, HEADS, HEAD_DIM), F32)
    return pl.pallas_call(
        functools.partial(_qkv_body, scale=scale),
        grid=(rows // tm,),
        in_specs=[pl.BlockSpec((tm, D), row), _resident((1, D)), _resident((D, 3 * D))],
        out_specs=[pl.BlockSpec((tm, D), row), heads_spec, heads_spec,
                   pl.BlockSpec((tm, D), row), pl.BlockSpec((tm, D), row)],
        out_shape=[out_sds(BF16), heads_sds, heads_sds, out_sds(BF16), out_sds(BF16)],
        compiler_params=_params(("parallel",)),
        name="qkv_proj",
    )(x, g.reshape(1, D), w)


def _sb_attn_body(q_ref, k_ref, v_ref, tri_ref, o_ref, *, tq, q_off):
    q0 = (pl.program_id(2) + q_off) * tq
    first_diag = lax.div(q0, KEY_TILE)
    n_diag = max(1, tq // KEY_TILE)
    lane = lax.broadcasted_iota(jnp.int32, (tq, LANES), 1)
    diff = (lax.broadcasted_iota(jnp.int32, (tq, KEY_TILE), 1)
            - lax.broadcasted_iota(jnp.int32, (tq, KEY_TILE), 0))
    col_minus_row = jnp.concatenate([diff, diff], axis=1)
    q_pair = q_ref[0]
    tri = tri_ref[...]

    def tile(j, masked):
        k0 = pl.multiple_of(j * KEY_TILE, KEY_TILE)
        z = _dot_nt(q_pair, _rows_by_head(k_ref[0, pl.ds(k0, KEY_TILE), :]))
        neg_abs = pltpu.bitcast(pltpu.bitcast(z, jnp.uint32) | jnp.uint32(0x80000000), F32)
        drop = jnp.maximum(z, 0.0) + jnp.log(1.0 + jnp.exp2(neg_abs)) * LOG2E
        if masked:
            valid = col_minus_row < (q0 - k0)
            drop = jnp.where(valid, drop, 0.0)
        hi = drop.astype(BF16)
        later = jnp.concatenate([_dot(hi[:, :KEY_TILE], tri), _dot(hi[:, KEY_TILE:], tri)], axis=1)
        w = jnp.exp2((z - drop) - later)
        if masked:
            w = jnp.where(valid, w, 0.0)
        o = _dot(w.astype(BF16), _rows_by_head(v_ref[0, pl.ds(k0, KEY_TILE), :]))
        total = jnp.where(lane < HEAD_DIM, jnp.sum(drop[:, :KEY_TILE], axis=-1, keepdims=True),
                          jnp.sum(drop[:, KEY_TILE:], axis=-1, keepdims=True))
        return o, total

    def step(tiles, carry, masked):
        dropped, out = carry
        for j in tiles:
            o, total = tile(j, masked)
            out = out + jnp.exp2(-dropped) * o
            dropped = dropped + total
        return dropped, out

    def alive(carry):
        return (jnp.min(carry[0]) < SB_DEAD_BITS).astype(jnp.int32)

    def more(state):
        t, live, _ = state
        return jnp.logical_and(t < first_diag, live > 0)

    def farther(state):
        t, _, carry = state
        carry = step([first_diag - 1 - t], carry, False)
        return t + 1, alive(carry), carry

    carry = (jnp.zeros((tq, LANES), F32), jnp.zeros((tq, LANES), F32))
    carry = step([first_diag + d for d in reversed(range(n_diag))], carry, True)
    carry = lax.while_loop(more, farther, (jnp.int32(0), alive(carry), carry))[2]
    o_ref[0] = carry[1].astype(BF16)


def _tri():
    j = lax.broadcasted_iota(jnp.int32, (KEY_TILE, KEY_TILE), 0)
    s = lax.broadcasted_iota(jnp.int32, (KEY_TILE, KEY_TILE), 1)
    return (j > s).astype(BF16)


def _sb_attn(q, k, v, *, tq, q_off):
    b, sq, _ = q.shape
    sk = k.shape[1]
    assert (q_off * tq) % (2 * KEY_TILE) == 0 and (tq % (2 * KEY_TILE) == 0 or sq == tq)
    assert (q_off + sq // tq - 1) * tq + max(tq, KEY_TILE) <= sk
    return pl.pallas_call(
        functools.partial(_sb_attn_body, tq=tq, q_off=q_off),
        grid=(b, HEADS // 2, sq // tq),
        in_specs=[pl.BlockSpec((1, tq, LANES), lambda bi, p, qi: (bi, qi, p)),
                  pl.BlockSpec((1, sk, LANES), lambda bi, p, qi: (bi, 0, p)),
                  pl.BlockSpec((1, sk, LANES), lambda bi, p, qi: (bi, 0, p)),
                  pl.BlockSpec((KEY_TILE, KEY_TILE), lambda bi, p, qi: (0, 0))],
        out_specs=pl.BlockSpec((1, tq, LANES), lambda bi, p, qi: (bi, qi, p)),
        out_shape=jax.ShapeDtypeStruct((b, sq, D), BF16),
        compiler_params=_params(("parallel", "parallel", "arbitrary")),
        name="sb_attn",
    )(q, k, v, _tri())


def _value_with_ones(v_tile, h):
    lane = lax.broadcasted_iota(jnp.int32, v_tile.shape, 1)
    return jnp.where(_head_lanes(lane, h), v_tile, jnp.ones_like(v_tile))


def _normalise_pair(accs):
    res = [a * pltpu.roll(1.0 / a, HEAD_DIM, 1) for a in accs]
    lane = lax.broadcasted_iota(jnp.int32, res[0].shape, 1)
    return jnp.where(lane < HEAD_DIM, res[0], res[1])


def _softmax_attn_body(q_ref, k_ref, v_ref, o_ref, *, tq, q_off, blocks):
    first = pl.program_id(2) * blocks + q_off
    chunk_diff = (_chunk_of(lax.broadcasted_iota(jnp.int32, (tq, SM_KEY_TILE), 1))
                  - _chunk_of(lax.broadcasted_iota(jnp.int32, (tq, SM_KEY_TILE), 0)))

    def chain(r0, n, h, j, state, visible):
        m, acc = state
        k0 = pl.multiple_of(j * SM_KEY_TILE, SM_KEY_TILE)
        s = _dot_nt(q_ref[0, r0:r0 + n, LANES * h:LANES * (h + 1)],
                    k_ref[0, pl.ds(k0, SM_KEY_TILE), LANES * h:LANES * (h + 1)])
        if visible is not None:
            s = jnp.where(visible, s, NEG)
        m_new = jnp.maximum(m, jnp.max(s, axis=-1, keepdims=True))
        p = jnp.exp2(s - m_new)
        v_pair = v_ref[0, pl.ds(k0, SM_KEY_TILE), LANES * (h // 2):LANES * (h // 2 + 1)]
        return m_new, acc * jnp.exp2(m - m_new) + _dot(p.astype(BF16), _value_with_ones(v_pair, h % 2))

    def full_tiles(n, tiles, carry):
        new = list(carry)
        for j in tiles:
            for h in range(SM_HEADS):
                new[h] = chain(0, n, h, j, new[h], None)
        return tuple(new)

    n = blocks * tq
    carry = tuple((jnp.full((n, 1), NEG, F32), jnp.zeros((n, LANES), F32)) for _ in range(SM_HEADS))
    if blocks == 1:
        last = lax.div(first * tq, SM_KEY_TILE)
        pairs = lax.div(last, 2)
        carry = lax.fori_loop(0, pairs, lambda t, c: full_tiles(n, [2 * t, 2 * t + 1], c), carry)
        carry = lax.fori_loop(2 * pairs, last, lambda t, c: full_tiles(n, [t], c), carry)
        own = chunk_diff <= _chunk_of(first * tq - last * SM_KEY_TILE)
        carry = tuple(chain(0, n, h, last, carry[h], own) for h in range(SM_HEADS))
    else:
        carry = lax.fori_loop(0, lax.div(first, 2), lambda t, c: full_tiles(n, [2 * t, 2 * t + 1], c), carry)
        for i in range(blocks):
            r0 = i * tq
            later_blocks = [jnp.full_like(chunk_diff, -1)] * (blocks - 1 - i)
            visible = jnp.concatenate([chunk_diff] + later_blocks, axis=0) <= 0
            stacked = []
            for h in range(SM_HEADS):
                m, acc = carry[h]
                m1, acc1 = chain(r0, n - r0, h, first + i, (m[r0:], acc[r0:]), visible)
                if r0:
                    m1, acc1 = jnp.concatenate([m[:r0], m1], axis=0), jnp.concatenate([acc[:r0], acc1], axis=0)
                stacked.append((m1, acc1))
            carry = tuple(stacked)
    for pair in range(SM_HEADS // 2):
        o_ref[0, :, LANES * pair:LANES * (pair + 1)] = _normalise_pair(
            [carry[2 * pair][1], carry[2 * pair + 1][1]]).astype(BF16)


def _softmax_attn(q, k, v, *, tq, q_off):
    b, sq, _ = q.shape
    sk = k.shape[1]
    assert tq <= SM_KEY_TILE and SM_KEY_TILE % tq == 0 and sk % SM_KEY_TILE == 0
    paired = tq == SM_KEY_TILE and (sq // tq) % SM_BLOCKS == 0 and q_off % 2 == 0
    blocks = SM_BLOCKS if paired else 1
    qk_lanes, v_lanes = SM_HEADS * LANES, SM_HEADS * HEAD_DIM
    return pl.pallas_call(
        functools.partial(_softmax_attn_body, tq=tq, q_off=q_off, blocks=blocks),
        grid=(b, HEADS // SM_HEADS, sq // (tq * blocks)),
        in_specs=[pl.BlockSpec((1, tq * blocks, qk_lanes), lambda bi, p, qi: (bi, qi, p)),
                  pl.BlockSpec((1, sk, qk_lanes), lambda bi, p, qi: (bi, 0, p)),
                  pl.BlockSpec((1, sk, v_lanes), lambda bi, p, qi: (bi, 0, p))],
        out_specs=pl.BlockSpec((1, tq * blocks, v_lanes), lambda bi, p, qi: (bi, qi, p)),
        out_shape=jax.ShapeDtypeStruct((b, sq, D), BF16),
        compiler_params=_params(("parallel", "parallel", "arbitrary")),
        name="mla_attn",
    )(q, k, v)


BAND_TQ = 4 * CHUNK


BAND_KEYS = 3 * KEY_TILE
BAND_BLOCKS = 4


def _band_attn_body(q_ref, k_ref, v_ref, bias_ref, o_ref, *, q_off, blocks):
    row = lax.broadcasted_iota(jnp.int32, (2 * BAND_KEYS, LANES), 0)
    lane = lax.broadcasted_iota(jnp.int32, (2 * BAND_KEYS, LANES), 1)
    ones = jnp.where((row < BAND_KEYS) == (lane < HEAD_DIM), 1.0, 0.0).astype(BF16)
    for g in range(blocks):
        rows = slice(BAND_TQ * g, BAND_TQ * (g + 1))
        qi = pl.program_id(2) * blocks + g + q_off
        k0 = pl.multiple_of(jnp.maximum(qi - 2, 0) * KEY_TILE, KEY_TILE)
        k_win = k_ref[0, pl.ds(k0, BAND_KEYS), :]
        v_win = v_ref[0, pl.ds(k0, BAND_KEYS), :]
        s = _dot_nt(q_ref[0, rows, :], _rows_by_head(k_win)) + bias_ref[0, jnp.minimum(qi, 2)]
        p = jnp.concatenate(
            [jnp.exp2(sh - jnp.max(sh, axis=-1, keepdims=True)) for sh in (s[:, :BAND_KEYS], s[:, BAND_KEYS:])],
            axis=1).astype(BF16)
        acc = _dot(p, jnp.concatenate([_rows_by_head(v_win), ones], axis=1))
        o_ref[0, rows, :] = (acc[:, :LANES] * (1.0 / acc[:, LANES:])).astype(BF16)


def _band_bias_tiles(rel_bias):
    table = rel_bias.astype(F32) * LOG2E
    heads = table.shape[0]
    n = BAND_TQ + KEY_TILE - 1
    tiles = []
    for t in range(3):
        d_lo = BAND_PAST - KEY_TILE * t - (KEY_TILE - 1)
        below = max(0, min(n, -REL_CLIP - d_lo))
        above = max(0, min(n, d_lo + n - 1 - REL_CLIP))
        mid_lo = d_lo + below + REL_CLIP
        diag = jnp.concatenate([jnp.broadcast_to(table[:, :1], (heads, below)),
                                table[:, mid_lo:mid_lo + n - below - above],
                                jnp.broadcast_to(table[:, -1:], (heads, above))], axis=1)
        rows = jnp.tile(diag, (1, BAND_TQ + 1))[:, :BAND_TQ * (n + 1)].reshape(heads, BAND_TQ, n + 1)
        tiles.append(rows[:, :, KEY_TILE - 1::-1])
    return jnp.stack(tiles, axis=1)


def _band_bias_layouts(rel_bias):
    tiles = _band_bias_tiles(rel_bias)
    heads = tiles.shape[0]
    row_chunk = (jnp.arange(BAND_TQ, dtype=jnp.int32) // CHUNK)[:, None]
    col_chunk = (jnp.arange(KEY_TILE, dtype=jnp.int32) // CHUNK)[None, :]
    eight_back = jnp.where(row_chunk <= col_chunk, tiles[:, 0], NEG)
    four_back = tiles[:, 1]
    own = jnp.where(col_chunk <= row_chunk, tiles[:, 2], NEG)
    ahead = jnp.full_like(own, NEG)
    layouts = [jnp.concatenate(slots, axis=-1) for slots in
               ((own, ahead, ahead), (four_back, own, ahead), (eight_back, four_back, own))]
    per_head = jnp.stack(layouts, axis=1)
    pairs = per_head.reshape(heads // 2, 2, 3, BAND_TQ, BAND_KEYS)
    return jnp.concatenate([pairs[:, 0], pairs[:, 1]], axis=-1)


def _band_attn(q, k, v, bias_layouts, *, q_off):
    b, sq, _ = q.shape
    sk = k.shape[1]
    assert sk >= BAND_KEYS and (q_off + sq // BAND_TQ) * BAND_TQ <= sk
    blocks = math.gcd(sq // BAND_TQ, BAND_BLOCKS)
    rows = blocks * BAND_TQ
    return pl.pallas_call(
        functools.partial(_band_attn_body, q_off=q_off, blocks=blocks),
        grid=(HEADS // 2, b, sq // rows),
        in_specs=[pl.BlockSpec((1, rows, LANES), lambda p, bi, qi: (bi, qi, p)),
                  pl.BlockSpec((1, sk, LANES), lambda p, bi, qi: (bi, 0, p)),
                  pl.BlockSpec((1, sk, LANES), lambda p, bi, qi: (bi, 0, p)),
                  pl.BlockSpec((1, 3, BAND_TQ, 2 * BAND_KEYS), lambda p, bi, qi: (p, 0, 0, 0))],
        out_specs=pl.BlockSpec((1, rows, LANES), lambda p, bi, qi: (bi, qi, p)),
        out_shape=jax.ShapeDtypeStruct((b, sq, D), BF16),
        compiler_params=_params(("parallel", "parallel", "arbitrary")),
        name="band_attn",
    )(q, k, v, bias_layouts)


def _mla_proj_body(x_ref, g_ref, wdq_ref, gq_ref, wq_ref, wkv_ref, gkv_ref,
                   qcos_ref, qsin_ref, kcos_ref, ksin_ref, q_ref, ckv_ref, kpe_ref):
    h = _rms(x_ref[...], g_ref[...]).astype(BF16)
    cq = _rms(_dot(h, wdq_ref[...]), gq_ref[...]).astype(BF16)
    qq = _dot(cq, wq_ref[...])
    qcos, qsin = qcos_ref[...], qsin_ref[...]
    half = HEADS * LANES
    for hd in range(HEADS):
        lo = LANES * hd
        q_ref[:, lo:lo + LANES] = (qq[:, lo:lo + LANES] * qcos
                                   + qq[:, half + lo:half + lo + LANES] * qsin).astype(BF16)
    kv = _dot(h, wkv_ref[...])
    ckv_ref[...] = _rms(kv[:, 0:LANES], gkv_ref[...])
    kpe_ref[...] = kv[:, LANES:2 * LANES] * kcos_ref[...] + kv[:, 2 * LANES:3 * LANES] * ksin_ref[...]


def _rope_tables(pos):
    half = MLA_ROPE // 2
    inv = ROPE_THETA ** (-jnp.arange(half, dtype=F32) / half)
    ang = pos.astype(F32)[:, None] * inv[None, :]
    cos = jnp.concatenate([jnp.cos(ang)] * 2, axis=-1)
    sin = jnp.concatenate([jnp.sin(ang)] * 2, axis=-1)
    n = pos.shape[0]
    pad = jnp.zeros((n, LANES - MLA_NOPE - MLA_ROPE), F32)
    q_scale = MLA_SCALE * LOG2E
    qcos = jnp.concatenate([jnp.ones((n, MLA_NOPE), F32), cos, pad], axis=-1) * q_scale
    qsin = jnp.concatenate([jnp.zeros((n, MLA_NOPE), F32), sin, pad], axis=-1) * q_scale
    kpad = jnp.zeros((n, LANES - MLA_ROPE), F32)
    kcos = jnp.concatenate([cos, kpad], axis=-1)
    ksin = jnp.concatenate([sin, kpad], axis=-1)
    return qcos, qsin, kcos, ksin


def _rotary_partner(w):
    half = MLA_ROPE // 2
    return jnp.concatenate([-w[..., half:], w[..., :half]], axis=-1)


def _mla_weights(wuq, wdkv, wuk):
    r = wuq.shape[0]
    wq3 = wuq.reshape(r, HEADS, MLA_NOPE + MLA_ROPE)
    zpad = jnp.zeros((r, HEADS, LANES - MLA_NOPE - MLA_ROPE), wuq.dtype)
    plain = jnp.concatenate([wq3, zpad], axis=-1).reshape(r, HEADS * LANES)
    partner = jnp.concatenate([jnp.zeros((r, HEADS, MLA_NOPE), wuq.dtype),
                               _rotary_partner(wq3[..., MLA_NOPE:]), zpad], axis=-1)
    wq = jnp.concatenate([plain, partner.reshape(r, HEADS * LANES)], axis=-1).astype(BF16)
    rope_w = wdkv[:, MLA_KV_RANK:]
    kz = jnp.zeros((D, LANES - MLA_ROPE), wdkv.dtype)
    wkv = jnp.concatenate([wdkv[:, :MLA_KV_RANK], rope_w, kz, _rotary_partner(rope_w), kz],
                          axis=-1).astype(BF16)
    top = jnp.concatenate([wuk.reshape(MLA_KV_RANK, HEADS, MLA_NOPE),
                           jnp.zeros((MLA_KV_RANK, HEADS, LANES - MLA_NOPE), wuk.dtype)], axis=-1)
    place = jnp.concatenate([jnp.zeros((MLA_ROPE, MLA_NOPE), F32), jnp.eye(MLA_ROPE, dtype=F32),
                             jnp.zeros((MLA_ROPE, LANES - MLA_NOPE - MLA_ROPE), F32)], axis=-1)
    place = jnp.concatenate([place, jnp.zeros((LANES - MLA_ROPE, LANES), F32)], axis=0)
    bottom = jnp.broadcast_to(place[:, None, :], (LANES, HEADS, LANES))
    wexp = jnp.concatenate([top, bottom], axis=0).reshape(2 * LANES, HEADS * LANES).astype(BF16)
    return wq, wkv, wexp


def _mla_proj(x, g, wdq, gq, wq, wkv, gkv, tables, seq):
    rows = x.shape[0]
    tm = _row_tile(rows)
    if seq >= tm:
        n_pos_tiles = seq // tm
        pos_map = lambda i: (i % n_pos_tiles, 0)
    else:
        tables = [jnp.tile(t, (tm // seq, 1)) for t in tables]
        pos_map = lambda i: (0, 0)
    row = lambda i: (i, 0)
    return pl.pallas_call(
        _mla_proj_body,
        grid=(rows // tm,),
        in_specs=[pl.BlockSpec((tm, D), row), _resident((1, D)), _resident((D, MLA_Q_RANK)),
                  _resident((1, MLA_Q_RANK)), _resident((MLA_Q_RANK, 2 * HEADS * LANES)),
                  _resident((D, 3 * LANES)), _resident((1, MLA_KV_RANK))]
                 + [pl.BlockSpec((tm, LANES), pos_map)] * 4,
        out_specs=[pl.BlockSpec((tm, HEADS * LANES), row), pl.BlockSpec((tm, LANES), row),
                   pl.BlockSpec((tm, LANES), row)],
        out_shape=[jax.ShapeDtypeStruct((rows, HEADS * LANES), BF16),
                   jax.ShapeDtypeStruct((rows, MLA_KV_RANK), F32),
                   jax.ShapeDtypeStruct((rows, LANES), F32)],
        compiler_params=_params(("parallel",)),
        name="mla_proj",
    )(x, g.reshape(1, D), wdq, gq.reshape(1, MLA_Q_RANK), wq, wkv, gkv.reshape(1, MLA_KV_RANK), *tables)


def _mla_expand_body(ckv_ref, kpe_ref, wexp_ref, wuv_ref, k_ref, v_ref):
    c = ckv_ref[...].astype(BF16)
    cat = jnp.concatenate([c, kpe_ref[...].astype(BF16)], axis=-1)
    k_ref[...] = _dot(cat, wexp_ref[...]).astype(BF16)
    v_ref[...] = _dot(c, wuv_ref[...]).astype(BF16)


def _mla_expand(ckv, kpe, wexp, wuv):
    rows = ckv.shape[0]
    tm = _row_tile(rows)
    row = lambda i: (i, 0)
    return pl.pallas_call(
        _mla_expand_body,
        grid=(rows // tm,),
        in_specs=[pl.BlockSpec((tm, LANES), row), pl.BlockSpec((tm, LANES), row),
                  _resident((2 * LANES, HEADS * LANES)), _resident((MLA_KV_RANK, D))],
        out_specs=[pl.BlockSpec((tm, HEADS * LANES), row), pl.BlockSpec((tm, D), row)],
        out_shape=[jax.ShapeDtypeStruct((rows, HEADS * LANES), BF16),
                   jax.ShapeDtypeStruct((rows, D), BF16)],
        compiler_params=_params(("parallel",)),
        name="mla_expand",
    )(ckv, kpe, wexp, wuv)


CONV_ROWS = 32
LANE_COLS = D // LANES


def _conv_body(x_ref, g_ref, w1_ref, b1_ref, hist_ref, dw_ref, bdw_ref, lng_ref, lnb_ref,
               a_ref, st_ref, gbuf, ybuf, *, tm):
    @pl.when(pl.program_id(1) == 0)
    def _():
        for c in range(LANE_COLS):
            gbuf[c, 0:HIST_PAD, :] = hist_ref[0, :, LANES * c:LANES * (c + 1)]

    h = _rms(x_ref[0], g_ref[...]).astype(BF16)
    u = _dot(h, w1_ref[...]) + b1_ref[...]
    glu = u[:, :D] * _sigmoid(u[:, D:])
    for c in range(LANE_COLS):
        gbuf[c, HIST_PAD:HIST_PAD + tm, :] = glu[:, LANES * c:LANES * (c + 1)]

    first = HIST_PAD - CONV_HIST

    def rows(i, _):
        r0 = pl.multiple_of(i * CONV_ROWS, CONV_ROWS)
        for c in range(LANE_COLS):
            cols = slice(LANES * c, LANES * (c + 1))
            acc = jnp.broadcast_to(bdw_ref[:, cols], (CONV_ROWS, LANES))
            for w in range(CONV_WIDTH):
                acc = acc + dw_ref[w:w + 1, cols] * gbuf[c, pl.ds(r0 + first + w, CONV_ROWS), :]
            ybuf[pl.ds(r0, CONV_ROWS), cols] = acc
        return 0

    lax.fori_loop(0, tm // CONV_ROWS, rows, 0)

    y = ybuf[...]
    yc = y - jnp.mean(y, axis=-1, keepdims=True)
    var = jnp.mean(yc * yc, axis=-1, keepdims=True)
    yn = yc * lax.rsqrt(var + EPS) * lng_ref[...] + lnb_ref[...]
    a_ref[0] = (yn * _sigmoid(yn)).astype(BF16)

    for c in range(LANE_COLS):
        tail = gbuf[c, tm:tm + HIST_PAD, :]
        st_ref[0, :, LANES * c:LANES * (c + 1)] = tail
        gbuf[c, 0:HIST_PAD, :] = tail


def _conv(x, g, w1, b1, hist, dw, bdw, lng, lnb):
    b, s, _ = x.shape
    tm = _row_tile(s)
    hist = jnp.pad(hist.astype(F32), ((0, 0), (HIST_PAD - CONV_HIST, 0), (0, 0)))
    dw = jnp.pad(dw.astype(F32), ((0, HIST_PAD - CONV_WIDTH), (0, 0)))
    a, st = pl.pallas_call(
        functools.partial(_conv_body, tm=tm),
        grid=(b, s // tm),
        in_specs=[pl.BlockSpec((1, tm, D), lambda bi, t: (bi, t, 0)), _resident((1, D)),
                  _resident((D, 2 * D)), _resident((1, 2 * D)),
                  pl.BlockSpec((1, HIST_PAD, D), lambda bi, t: (bi, 0, 0)),
                  _resident((HIST_PAD, D)), _resident((1, D)), _resident((1, D)), _resident((1, D))],
        out_specs=[pl.BlockSpec((1, tm, D), lambda bi, t: (bi, t, 0)),
                   pl.BlockSpec((1, HIST_PAD, D), lambda bi, t: (bi, 0, 0))],
        out_shape=[jax.ShapeDtypeStruct((b, s, D), BF16),
                   jax.ShapeDtypeStruct((b, HIST_PAD, D), F32)],
        scratch_shapes=[pltpu.VMEM((LANE_COLS, HIST_PAD + tm, LANES), F32), pltpu.VMEM((tm, D), F32)],
        compiler_params=_params(("parallel", "arbitrary")),
        name="conv_module",
    )(x, g.reshape(1, D), w1, b1.reshape(1, 2 * D), hist, dw, bdw.reshape(1, D),
      lng.reshape(1, D), lnb.reshape(1, D))
    return a, st[:, HIST_PAD - CONV_HIST:]


def _post_body(x_ref, a_ref, p_ref, wo_ref, bo_ref, gf_ref, w13_ref, w2_ref, gp_ref, wg_ref, wp_ref,
               gfin_ref, o_ref, h_s, acc_s, *, final):
    x1 = x_ref[...] + _dot(a_ref[...], wo_ref[...]) + bo_ref[...]
    h_s[...] = _rms(x1, gf_ref[...]).astype(BF16)
    acc_s[...] = x1

    def gated(c):
        uv = _dot(h_s[...], w13_ref[c])
        u = uv[:, :FFN_CHUNK]
        return _dot((u * _sigmoid(u) * uv[:, FFN_CHUNK:]).astype(BF16), w2_ref[c])

    def chunk_pair(i, _):
        acc_s[...] += gated(2 * i) + gated(2 * i + 1)
        return 0

    lax.fori_loop(0, N_FFN_CHUNKS // 2, chunk_pair, 0)
    for c in range(N_FFN_CHUNKS // 2 * 2, N_FFN_CHUNKS):
        acc_s[...] += gated(c)
    x2 = acc_s[...]
    gate = _sigmoid(_dot(_rms(x2, gp_ref[...]).astype(BF16), wg_ref[...]))
    x3 = x2 + gate * _dot(p_ref[...].astype(BF16), wp_ref[...])
    o_ref[...] = _rms(x3, gfin_ref[...]) if final else x3


def _post(x, a, p, wo, bo, gf, w13, w2, gp, wg, wp, gfin, *, final):
    rows = x.shape[0]
    tm = _row_tile(rows)
    row = lambda i: (i, 0)
    return pl.pallas_call(
        functools.partial(_post_body, final=final),
        grid=(rows // tm,),
        in_specs=[pl.BlockSpec((tm, D), row), pl.BlockSpec((tm, D), row), pl.BlockSpec((tm, PLE_DIM), row),
                  _resident((D, D)), _resident((1, D)), _resident((1, D)),
                  _resident((N_FFN_CHUNKS, D, 2 * FFN_CHUNK)), _resident((N_FFN_CHUNKS, FFN_CHUNK, D)),
                  _resident((1, D)), _resident((D, D)), _resident((PLE_DIM, D)), _resident((1, D))],
        out_specs=pl.BlockSpec((tm, D), row),
        out_shape=jax.ShapeDtypeStruct((rows, D), F32),
        scratch_shapes=[pltpu.VMEM((tm, D), BF16), pltpu.VMEM((tm, D), F32)],
        compiler_params=_params(("parallel",)),
        name="post_mixer",
    )(x, a, p, wo, bo.reshape(1, D), gf.reshape(1, D), w13, w2, gp.reshape(1, D), wg, wp,
      gfin.reshape(1, D))


def _ffn_weights(w1, w3, w2):
    split = lambda w: w.reshape(D, N_FFN_CHUNKS, FFN_CHUNK).transpose(1, 0, 2)
    w13 = jnp.concatenate([split(w1), split(w3)], axis=-1).astype(BF16)
    return w13, w2.reshape(N_FFN_CHUNKS, FFN_CHUNK, D).astype(BF16)


def _with_cache(cache, new, total):
    b, past, w = cache.shape
    pad = jnp.zeros((b, total - past - new.shape[1], w), new.dtype)
    return jnp.concatenate([cache.astype(new.dtype), new, pad], axis=1)


def kernel(x_prompt, x_sample, p_prompt, p_sample, cache_sb_k, cache_sb_v, cache_mla_ckv, cache_mla_kpe, state_conv, cache_band_k, cache_band_v, norm_mix, norm_ffn, norm_ple, norm_final, ffn_w1, ffn_w3, ffn_w2, ple_proj, ple_gate, sb_wqkv, sb_wo, mla_wdq, mla_gq, mla_wuq, mla_wdkv, mla_gkv, mla_wuk, mla_wuv, mla_wo, conv_w1, conv_b1, conv_dw, conv_bdw, conv_ln_g, conv_ln_b, conv_w2, conv_b2, band_wqkv, band_rel_bias, band_wo):
    bp, s, _ = x_prompt.shape
    bs, t, _ = x_sample.shape
    depth = norm_mix.shape[0]
    past = cache_sb_k.shape[2]
    xp = x_prompt.reshape(bp * s, D)
    xs = x_sample.reshape(bs * t, D)
    pos_p = jnp.arange(s, dtype=jnp.int32)
    pos_s = past + jnp.arange(t, dtype=jnp.int32)
    zero_bias = jnp.zeros((D,), F32)
    head_scale = HEAD_DIM ** -0.5 * LOG2E
    sample_keys = -(-(past + t) // KEY_TILE) * KEY_TILE
    mla_keys = -(-(past + t) // SM_KEY_TILE) * SM_KEY_TILE

    outs = {name: [] for name in ("sb_kp", "sb_vp", "sb_ks", "sb_vs", "mla_cp", "mla_pp", "mla_cs", "mla_ps",
                                  "conv_p", "conv_s", "band_kp", "band_vp", "band_ks", "band_vs")}

    for i in range(depth):
        m, j = i % N_MIXERS, i // N_MIXERS
        if m == 0:
            w = sb_wqkv[j].astype(BF16)
            q, kf, vf, kb, vb = _qkv(xp, norm_mix[i], w, head_scale)
            ap = _sb_attn(q.reshape(bp, s, D), kb.reshape(bp, s, D), vb.reshape(bp, s, D),
                          tq=ROW_TILE, q_off=0)
            outs["sb_kp"].append(kf.reshape(bp, s, HEADS, HEAD_DIM))
            outs["sb_vp"].append(vf.reshape(bp, s, HEADS, HEAD_DIM))
            q, kf, vf, kb, vb = _qkv(xs, norm_mix[i], w, head_scale)
            k_all = _with_cache(cache_sb_k[j].reshape(bs, past, D), kb.reshape(bs, t, D), sample_keys)
            v_all = _with_cache(cache_sb_v[j].reshape(bs, past, D), vb.reshape(bs, t, D), sample_keys)
            as_ = _sb_attn(q.reshape(bs, t, D), k_all, v_all, tq=t, q_off=past // t)
            outs["sb_ks"].append(kf.reshape(bs, t, HEADS, HEAD_DIM))
            outs["sb_vs"].append(vf.reshape(bs, t, HEADS, HEAD_DIM))
            wo, bo = sb_wo[j].astype(BF16), zero_bias
        elif m == 1:
            wq, wkv, wexp = _mla_weights(mla_wuq[j], mla_wdkv[j], mla_wuk[j])
            wdq, wuv = mla_wdq[j].astype(BF16), mla_wuv[j].astype(BF16)
            q, ckv, kpe = _mla_proj(xp, norm_mix[i], wdq, mla_gq[j], wq, wkv, mla_gkv[j],
                                    _rope_tables(pos_p), s)
            k, v = _mla_expand(ckv, kpe, wexp, wuv)
            ap = _softmax_attn(q.reshape(bp, s, -1), k.reshape(bp, s, -1), v.reshape(bp, s, D),
                               tq=ROW_TILE, q_off=0)
            outs["mla_cp"].append(ckv.reshape(bp, s, MLA_KV_RANK))
            outs["mla_pp"].append(kpe[:, :MLA_ROPE].reshape(bp, s, MLA_ROPE))
            q, ckv, kpe = _mla_proj(xs, norm_mix[i], wdq, mla_gq[j], wq, wkv, mla_gkv[j],
                                    _rope_tables(pos_s), t)
            ckv_all = _with_cache(cache_mla_ckv[j], ckv.reshape(bs, t, MLA_KV_RANK), mla_keys)
            kpe_cache = jnp.pad(cache_mla_kpe[j], ((0, 0), (0, 0), (0, LANES - MLA_ROPE)))
            kpe_all = _with_cache(kpe_cache, kpe.reshape(bs, t, LANES), mla_keys)
            k, v = _mla_expand(ckv_all.reshape(bs * mla_keys, MLA_KV_RANK),
                               kpe_all.reshape(bs * mla_keys, LANES), wexp, wuv)
            as_ = _softmax_attn(q.reshape(bs, t, -1), k.reshape(bs, mla_keys, -1),
                                v.reshape(bs, mla_keys, D), tq=t, q_off=past // t)
            outs["mla_cs"].append(ckv.reshape(bs, t, MLA_KV_RANK))
            outs["mla_ps"].append(kpe[:, :MLA_ROPE].reshape(bs, t, MLA_ROPE))
            wo, bo = mla_wo[j].astype(BF16), zero_bias
        elif m == 2:
            w1 = conv_w1[j].astype(BF16)
            ap, st = _conv(xp.reshape(bp, s, D), norm_mix[i], w1, conv_b1[j],
                           jnp.zeros((bp, CONV_HIST, D), F32), conv_dw[j], conv_bdw[j],
                           conv_ln_g[j], conv_ln_b[j])
            outs["conv_p"].append(st)
            as_, st = _conv(xs.reshape(bs, t, D), norm_mix[i], w1, conv_b1[j], state_conv[j],
                            conv_dw[j], conv_bdw[j], conv_ln_g[j], conv_ln_b[j])
            outs["conv_s"].append(st)
            wo, bo = conv_w2[j].astype(BF16), conv_b2[j]
        else:
            w = band_wqkv[j].astype(BF16)
            bias_tiles = _band_bias_layouts(band_rel_bias[j])
            q, kf, vf, kb, vb = _qkv(xp, norm_mix[i], w, head_scale)
            ap = _band_attn(q.reshape(bp, s, D), kb.reshape(bp, s, D), vb.reshape(bp, s, D),
                            bias_tiles, q_off=0)
            keep = min(BAND_PAST, s)
            outs["band_kp"].append(kf.reshape(bp, s, HEADS, HEAD_DIM)[:, s - keep:])
            outs["band_vp"].append(vf.reshape(bp, s, HEADS, HEAD_DIM)[:, s - keep:])
            q, kf, vf, kb, vb = _qkv(xs, norm_mix[i], w, head_scale)
            n_past = cache_band_k.shape[2]
            local = n_past + BAND_TQ
            k_all = _with_cache(cache_band_k[j].reshape(bs, n_past, D), kb.reshape(bs, t, D), local)
            v_all = _with_cache(cache_band_v[j].reshape(bs, n_past, D), vb.reshape(bs, t, D), local)
            q_pad = jnp.pad(q.reshape(bs, t, D), ((0, 0), (0, BAND_TQ - t), (0, 0)))
            as_ = _band_attn(q_pad, k_all, v_all, bias_tiles, q_off=n_past // BAND_TQ)[:, :t]
            kk = jnp.concatenate([cache_band_k[j], kf.reshape(bs, t, HEADS, HEAD_DIM)], axis=1)
            vv = jnp.concatenate([cache_band_v[j], vf.reshape(bs, t, HEADS, HEAD_DIM)], axis=1)
            outs["band_ks"].append(kk[:, t:])
            outs["band_vs"].append(vv[:, t:])
            wo, bo = band_wo[j].astype(BF16), zero_bias

        w13, w2 = _ffn_weights(ffn_w1[i], ffn_w3[i], ffn_w2[i])
        final = i == depth - 1
        post = functools.partial(_post, wo=wo, bo=bo, gf=norm_ffn[i], w13=w13, w2=w2, gp=norm_ple[i],
                                 wg=ple_gate[i].astype(BF16), wp=ple_proj[i].astype(BF16),
                                 gfin=norm_final, final=final)
        xp = post(xp, ap.reshape(bp * s, D), p_prompt[i].reshape(bp * s, PLE_DIM))
        xs = post(xs, as_.reshape(bs * t, D), p_sample[i].reshape(bs * t, PLE_DIM))

    stack = lambda name: jnp.stack(outs[name])
    return (xp.reshape(bp, s, D), xs.reshape(bs, t, D),
            stack("sb_kp"), stack("sb_vp"), stack("sb_ks"), stack("sb_vs"),
            stack("mla_cp"), stack("mla_pp"), stack("mla_cs"), stack("mla_ps"),
            stack("conv_p"), stack("conv_s"),
            stack("band_kp"), stack("band_vp"), stack("band_ks"), stack("band_vs"))
```

```python
import functools
import math

import jax
import jax.numpy as jnp
from jax import lax
from jax.experimental import pallas as pl
from jax.experimental.pallas import tpu as pltpu

F32 = jnp.float32
BF16 = jnp.bfloat16

D = 1024
EPS = 1e-6
NEG = -1e30
HEADS = 16
HEAD_DIM = 64
CHUNK = 64
PLE_DIM = 256
FFN_HIDDEN = 2816
FFN_CHUNK = 256
N_FFN_CHUNKS = FFN_HIDDEN // FFN_CHUNK
MLA_Q_RANK = 256
MLA_KV_RANK = 128
MLA_NOPE = 64
MLA_ROPE = 32
MLA_SCALE = (MLA_NOPE + MLA_ROPE) ** -0.5
ROPE_THETA = 10000.0
CONV_WIDTH = 31
CONV_HIST = CONV_WIDTH - 1
HIST_PAD = 32
BAND_CHUNKS = 8
BAND_PAST = BAND_CHUNKS * CHUNK
REL_CLIP = 128
N_MIXERS = 4

LANES = 128
KEY_TILE = 256
SB_DEAD_BITS = 160.0
SM_KEY_TILE = 2 * KEY_TILE
SM_HEADS = 2
SM_BLOCKS = 2
LOG2E = math.log2(math.e)
ROW_TILE = 512
VMEM_LIMIT = 56 * 1024 * 1024


def _dot(a, b):
    return jnp.dot(a, b, preferred_element_type=F32)


def _dot_nt(a, b):
    return lax.dot_general(a, b, (((1,), (1,)), ((), ())), preferred_element_type=F32)


def _rms(x, g):
    return x * lax.rsqrt(jnp.mean(x * x, axis=-1, keepdims=True) + EPS) * g


def _sigmoid(x):
    return 1.0 / (1.0 + jnp.exp(-x))


def _params(sem):
    return pltpu.CompilerParams(dimension_semantics=sem, vmem_limit_bytes=VMEM_LIMIT)


def _resident(shape):
    zeros = (0,) * len(shape)
    return pl.BlockSpec(shape, lambda *_: zeros, pipeline_mode=pl.Buffered(1))


def _row_tile(rows):
    return min(ROW_TILE, rows)


def _chunk_of(idx):
    return lax.shift_right_arithmetic(idx, CHUNK.bit_length() - 1)


def _head_lanes(lane, h):
    return lane < HEAD_DIM if h == 0 else lane >= HEAD_DIM


def _rows_by_head(rows):
    lane = lax.broadcasted_iota(jnp.int32, rows.shape, 1)
    zero = jnp.zeros_like(rows)
    return jnp.concatenate([jnp.where(_head_lanes(lane, h), rows, zero) for h in range(2)], axis=0)


def _qkv_body(x_ref, g_ref, w_ref, q_ref, kf_ref, vf_ref, kb_ref, vb_ref, *, scale):
    h = _rms(x_ref[...], g_ref[...]).astype(BF16)
    q_ref[...] = (_dot(h, w_ref[:, 0:D]) * scale).astype(BF16)
    k = _dot(h, w_ref[:, D:2 * D])
    kf_ref[...] = k.reshape(kf_ref.shape)
    kb_ref[...] = k.astype(BF16)
    v = _dot(h, w_ref[:, 2 * D:3 * D])
    vf_ref[...] = v.reshape(vf_ref.shape)
    vb_ref[...] = v.astype(BF16)


def _qkv(x, g, w, scale):
    rows = x.shape[0]
    tm = _row_tile(rows)
    row = lambda i: (i, 0)
    out_sds = lambda dt: jax.ShapeDtypeStruct((rows, D), dt)
    heads_spec = pl.BlockSpec((tm, HEADS, HEAD_DIM), lambda i: (i, 0, 0))
    heads_sds = jax.ShapeDtypeStruct((rows, HEADS, HEAD_DIM), F32)
    return pl.pallas_call(
        functools.partial(_qkv_body, scale=scale),
        grid=(rows // tm,),
        in_specs=[pl.BlockSpec((tm, D), row), _resident((1, D)), _resident((D, 3 * D))],
        out_specs=[pl.BlockSpec((tm, D), row), heads_spec, heads_spec,
                   pl.BlockSpec((tm, D), row), pl.BlockSpec((tm, D), row)],
        out_shape=[out_sds(BF16), heads_sds, heads_sds, out_sds(BF16), out_sds(BF16)],
        compiler_params=_params(("parallel",)),
        name="qkv_proj",
    )(x, g.reshape(1, D), w)


def _sb_attn_body(q_ref, k_ref, v_ref, tri_ref, o_ref, *, tq, q_off):
    q0 = (pl.program_id(2) + q_off) * tq
    rows = min(tq, KEY_TILE)
    lane = lax.broadcasted_iota(jnp.int32, (rows, LANES), 1)
    diff = (lax.broadcasted_iota(jnp.int32, (rows, KEY_TILE), 1)
            - lax.broadcasted_iota(jnp.int32, (rows, KEY_TILE), 0))
    col_minus_row = jnp.concatenate([diff, diff], axis=1)
    tri = tri_ref[...]

    def tile(r0, j, masked):
        k0 = pl.multiple_of(j * KEY_TILE, KEY_TILE)
        z = _dot_nt(q_ref[0, r0:r0 + rows, :], _rows_by_head(k_ref[0, pl.ds(k0, KEY_TILE), :]))
        neg_abs = pltpu.bitcast(pltpu.bitcast(z, jnp.uint32) | jnp.uint32(0x80000000), F32)
        drop = jnp.maximum(z, 0.0) + jnp.log(1.0 + jnp.exp2(neg_abs)) * LOG2E
        if masked:
            valid = col_minus_row < (q0 + r0 - k0)
            drop = jnp.where(valid, drop, 0.0)
        hi = drop.astype(BF16)
        later = jnp.concatenate([_dot(hi[:, :KEY_TILE], tri), _dot(hi[:, KEY_TILE:], tri)], axis=1)
        w = jnp.exp2((z - drop) - later)
        if masked:
            w = jnp.where(valid, w, 0.0)
        o = _dot(w.astype(BF16), _rows_by_head(v_ref[0, pl.ds(k0, KEY_TILE), :]))
        total = jnp.where(lane < HEAD_DIM, jnp.sum(drop[:, :KEY_TILE], axis=-1, keepdims=True),
                          jnp.sum(drop[:, KEY_TILE:], axis=-1, keepdims=True))
        return o, total

    def add_tile(r0, j, carry, masked):
        dropped, out = carry
        o, total = tile(r0, j, masked)
        return dropped + total, out + jnp.exp2(-dropped) * o

    def alive(carry):
        return (jnp.min(carry[0]) < SB_DEAD_BITS).astype(jnp.int32)

    for r0 in range(0, tq, rows):
        own = lax.div(q0 + r0, KEY_TILE)
        carry = add_tile(r0, own, (jnp.zeros((rows, LANES), F32), jnp.zeros((rows, LANES), F32)), True)

        def more(state, own=own):
            t, live, _ = state
            return jnp.logical_and(t < own, live > 0)

        def farther(state, own=own, r0=r0):
            t, _, carry = state
            carry = add_tile(r0, own - 1 - t, carry, False)
            return t + 1, alive(carry), carry

        carry = lax.while_loop(more, farther, (jnp.int32(0), alive(carry), carry))[2]
        o_ref[0, r0:r0 + rows, :] = carry[1].astype(BF16)


def _tri():
    j = lax.broadcasted_iota(jnp.int32, (KEY_TILE, KEY_TILE), 0)
    s = lax.broadcasted_iota(jnp.int32, (KEY_TILE, KEY_TILE), 1)
    return (j > s).astype(BF16)


def _sb_attn(q, k, v, *, tq, q_off):
    b, sq, _ = q.shape
    sk = k.shape[1]
    assert (q_off * tq) % (2 * KEY_TILE) == 0 and (tq % (2 * KEY_TILE) == 0 or sq == tq)
    assert (q_off + sq // tq - 1) * tq + max(tq, KEY_TILE) <= sk
    return pl.pallas_call(
        functools.partial(_sb_attn_body, tq=tq, q_off=q_off),
        grid=(b, HEADS // 2, sq // tq),
        in_specs=[pl.BlockSpec((1, tq, LANES), lambda bi, p, qi: (bi, qi, p)),
                  pl.BlockSpec((1, sk, LANES), lambda bi, p, qi: (bi, 0, p)),
                  pl.BlockSpec((1, sk, LANES), lambda bi, p, qi: (bi, 0, p)),
                  pl.BlockSpec((KEY_TILE, KEY_TILE), lambda bi, p, qi: (0, 0))],
        out_specs=pl.BlockSpec((1, tq, LANES), lambda bi, p, qi: (bi, qi, p)),
        out_shape=jax.ShapeDtypeStruct((b, sq, D), BF16),
        compiler_params=_params(("parallel", "parallel", "arbitrary")),
        name="sb_attn",
    )(q, k, v, _tri())


def _value_with_ones(v_tile, h):
    lane = lax.broadcasted_iota(jnp.int32, v_tile.shape, 1)
    return jnp.where(_head_lanes(lane, h), v_tile, jnp.ones_like(v_tile))


def _normalise_pair(accs):
    res = [a * pltpu.roll(1.0 / a, HEAD_DIM, 1) for a in accs]
    lane = lax.broadcasted_iota(jnp.int32, res[0].shape, 1)
    return jnp.where(lane < HEAD_DIM, res[0], res[1])


def _softmax_attn_body(q_ref, k_ref, v_ref, o_ref, *, tq, q_off, blocks):
    first = pl.program_id(2) * blocks + q_off
    chunk_diff = (_chunk_of(lax.broadcasted_iota(jnp.int32, (tq, SM_KEY_TILE), 1))
                  - _chunk_of(lax.broadcasted_iota(jnp.int32, (tq, SM_KEY_TILE), 0)))

    def chain(r0, n, h, j, state, visible):
        m, acc = state
        k0 = pl.multiple_of(j * SM_KEY_TILE, SM_KEY_TILE)
        s = _dot_nt(q_ref[0, r0:r0 + n, LANES * h:LANES * (h + 1)],
                    k_ref[0, pl.ds(k0, SM_KEY_TILE), LANES * h:LANES * (h + 1)])
        if visible is not None:
            s = jnp.where(visible, s, NEG)
        m_new = jnp.maximum(m, jnp.max(s, axis=-1, keepdims=True))
        p = jnp.exp2(s - m_new)
        v_pair = v_ref[0, pl.ds(k0, SM_KEY_TILE), LANES * (h // 2):LANES * (h // 2 + 1)]
        return m_new, acc * jnp.exp2(m - m_new) + _dot(p.astype(BF16), _value_with_ones(v_pair, h % 2))

    def full_tiles(n, tiles, carry):
        new = list(carry)
        for j in tiles:
            for h in range(SM_HEADS):
                new[h] = chain(0, n, h, j, new[h], None)
        return tuple(new)

    n = blocks * tq
    carry = tuple((jnp.full((n, 1), NEG, F32), jnp.zeros((n, LANES), F32)) for _ in range(SM_HEADS))
    if blocks == 1:
        last = lax.div(first * tq, SM_KEY_TILE)
        pairs = lax.div(last, 2)
        carry = lax.fori_loop(0, pairs, lambda t, c: full_tiles(n, [2 * t, 2 * t + 1], c), carry)
        carry = lax.fori_loop(2 * pairs, last, lambda t, c: full_tiles(n, [t], c), carry)
        own = chunk_diff <= _chunk_of(first * tq - last * SM_KEY_TILE)
        carry = tuple(chain(0, n, h, last, carry[h], own) for h in range(SM_HEADS))
    else:
        carry = lax.fori_loop(0, lax.div(first, 2), lambda t, c: full_tiles(n, [2 * t, 2 * t + 1], c), carry)
        for i in range(blocks):
            r0 = i * tq
            later_blocks = [jnp.full_like(chunk_diff, -1)] * (blocks - 1 - i)
            visible = jnp.concatenate([chunk_diff] + later_blocks, axis=0) <= 0
            stacked = []
            for h in range(SM_HEADS):
                m, acc = carry[h]
                m1, acc1 = chain(r0, n - r0, h, first + i, (m[r0:], acc[r0:]), visible)
                if r0:
                    m1, acc1 = jnp.concatenate([m[:r0], m1], axis=0), jnp.concatenate([acc[:r0], acc1], axis=0)
                stacked.append((m1, acc1))
            carry = tuple(stacked)
    for pair in range(SM_HEADS // 2):
        o_ref[0, :, LANES * pair:LANES * (pair + 1)] = _normalise_pair(
            [carry[2 * pair][1], carry[2 * pair + 1][1]]).astype(BF16)


def _softmax_attn(q, k, v, *, tq, q_off):
    b, sq, _ = q.shape
    sk = k.shape[1]
    assert tq <= SM_KEY_TILE and SM_KEY_TILE % tq == 0 and sk % SM_KEY_TILE == 0
    paired = tq == SM_KEY_TILE and (sq // tq) % SM_BLOCKS == 0 and q_off % 2 == 0
    blocks = SM_BLOCKS if paired else 1
    qk_lanes, v_lanes = SM_HEADS * LANES, SM_HEADS * HEAD_DIM
    return pl.pallas_call(
        functools.partial(_softmax_attn_body, tq=tq, q_off=q_off, blocks=blocks),
        grid=(b, HEADS // SM_HEADS, sq // (tq * blocks)),
        in_specs=[pl.BlockSpec((1, tq * blocks, qk_lanes), lambda bi, p, qi: (bi, qi, p)),
                  pl.BlockSpec((1, sk, qk_lanes), lambda bi, p, qi: (bi, 0, p)),
                  pl.BlockSpec((1, sk, v_lanes), lambda bi, p, qi: (bi, 0, p))],
        out_specs=pl.BlockSpec((1, tq * blocks, v_lanes), lambda bi, p, qi: (bi, qi, p)),
        out_shape=jax.ShapeDtypeStruct((b, sq, D), BF16),
        compiler_params=_params(("parallel", "parallel", "arbitrary")),
        name="mla_attn",
    )(q, k, v)


BAND_TQ = 4 * CHUNK


BAND_KEYS = 3 * KEY_TILE
BAND_BLOCKS = 8


def _band_attn_body(q_ref, k_ref, v_ref, bias_ref, o_ref, *, q_off, blocks):
    row = lax.broadcasted_iota(jnp.int32, (2 * BAND_KEYS, LANES), 0)
    lane = lax.broadcasted_iota(jnp.int32, (2 * BAND_KEYS, LANES), 1)
    ones = jnp.where((row < BAND_KEYS) == (lane < HEAD_DIM), 1.0, 0.0).astype(BF16)
    for g in range(blocks):
        rows = slice(BAND_TQ * g, BAND_TQ * (g + 1))
        qi = pl.program_id(2) * blocks + g + q_off
        k0 = pl.multiple_of(jnp.maximum(qi - 2, 0) * KEY_TILE, KEY_TILE)
        k_win = k_ref[0, pl.ds(k0, BAND_KEYS), :]
        v_win = v_ref[0, pl.ds(k0, BAND_KEYS), :]
        s = _dot_nt(q_ref[0, rows, :], _rows_by_head(k_win)) + bias_ref[0, jnp.minimum(qi, 2)]
        p = jnp.concatenate(
            [jnp.exp2(sh - jnp.max(sh, axis=-1, keepdims=True)) for sh in (s[:, :BAND_KEYS], s[:, BAND_KEYS:])],
            axis=1).astype(BF16)
        acc = _dot(p, jnp.concatenate([_rows_by_head(v_win), ones], axis=1))
        o_ref[0, rows, :] = (acc[:, :LANES] * (1.0 / acc[:, LANES:])).astype(BF16)


def _band_bias_tiles(rel_bias):
    table = rel_bias.astype(F32) * LOG2E
    heads = table.shape[0]
    n = BAND_TQ + KEY_TILE - 1
    tiles = []
    for t in range(3):
        d_lo = BAND_PAST - KEY_TILE * t - (KEY_TILE - 1)
        below = max(0, min(n, -REL_CLIP - d_lo))
        above = max(0, min(n, d_lo + n - 1 - REL_CLIP))
        mid_lo = d_lo + below + REL_CLIP
        diag = jnp.concatenate([jnp.broadcast_to(table[:, :1], (heads, below)),
                                table[:, mid_lo:mid_lo + n - below - above],
                                jnp.broadcast_to(table[:, -1:], (heads, above))], axis=1)
        rows = jnp.tile(diag, (1, BAND_TQ + 1))[:, :BAND_TQ * (n + 1)].reshape(heads, BAND_TQ, n + 1)
        tiles.append(rows[:, :, KEY_TILE - 1::-1])
    return jnp.stack(tiles, axis=1)


def _band_bias_layouts(rel_bias):
    tiles = _band_bias_tiles(rel_bias)
    heads = tiles.shape[0]
    row_chunk = (jnp.arange(BAND_TQ, dtype=jnp.int32) // CHUNK)[:, None]
    col_chunk = (jnp.arange(KEY_TILE, dtype=jnp.int32) // CHUNK)[None, :]
    eight_back = jnp.where(row_chunk <= col_chunk, tiles[:, 0], NEG)
    four_back = tiles[:, 1]
    own = jnp.where(col_chunk <= row_chunk, tiles[:, 2], NEG)
    ahead = jnp.full_like(own, NEG)
    layouts = [jnp.concatenate(slots, axis=-1) for slots in
               ((own, ahead, ahead), (four_back, own, ahead), (eight_back, four_back, own))]
    per_head = jnp.stack(layouts, axis=1)
    pairs = per_head.reshape(heads // 2, 2, 3, BAND_TQ, BAND_KEYS)
    return jnp.concatenate([pairs[:, 0], pairs[:, 1]], axis=-1)


def _band_attn(q, k, v, bias_layouts, *, q_off):
    b, sq, _ = q.shape
    sk = k.shape[1]
    assert sk >= BAND_KEYS and (q_off + sq // BAND_TQ) * BAND_TQ <= sk
    blocks = math.gcd(sq // BAND_TQ, BAND_BLOCKS)
    rows = blocks * BAND_TQ
    return pl.pallas_call(
        functools.partial(_band_attn_body, q_off=q_off, blocks=blocks),
        grid=(HEADS // 2, b, sq // rows),
        in_specs=[pl.BlockSpec((1, rows, LANES), lambda p, bi, qi: (bi, qi, p)),
                  pl.BlockSpec((1, sk, LANES), lambda p, bi, qi: (bi, 0, p)),
                  pl.BlockSpec((1, sk, LANES), lambda p, bi, qi: (bi, 0, p)),
                  pl.BlockSpec((1, 3, BAND_TQ, 2 * BAND_KEYS), lambda p, bi, qi: (p, 0, 0, 0))],
        out_specs=pl.BlockSpec((1, rows, LANES), lambda p, bi, qi: (bi, qi, p)),
        out_shape=jax.ShapeDtypeStruct((b, sq, D), BF16),
        compiler_params=_params(("parallel", "parallel", "arbitrary")),
        name="band_attn",
    )(q, k, v, bias_layouts)


def _mla_proj_body(x_ref, g_ref, wdq_ref, gq_ref, wq_ref, wkv_ref, gkv_ref,
                   qcos_ref, qsin_ref, kcos_ref, ksin_ref, q_ref, ckv_ref, kpe_ref):
    h = _rms(x_ref[...], g_ref[...]).astype(BF16)
    cq = _rms(_dot(h, wdq_ref[...]), gq_ref[...]).astype(BF16)
    qq = _dot(cq, wq_ref[...])
    qcos, qsin = qcos_ref[...], qsin_ref[...]
    half = HEADS * LANES
    for hd in range(HEADS):
        lo = LANES * hd
        q_ref[:, lo:lo + LANES] = (qq[:, lo:lo + LANES] * qcos
                                   + qq[:, half + lo:half + lo + LANES] * qsin).astype(BF16)
    kv = _dot(h, wkv_ref[...])
    ckv_ref[...] = _rms(kv[:, 0:LANES], gkv_ref[...])
    kpe_ref[...] = kv[:, LANES:2 * LANES] * kcos_ref[...] + kv[:, 2 * LANES:3 * LANES] * ksin_ref[...]


def _rope_tables(pos):
    half = MLA_ROPE // 2
    inv = ROPE_THETA ** (-jnp.arange(half, dtype=F32) / half)
    ang = pos.astype(F32)[:, None] * inv[None, :]
    cos = jnp.concatenate([jnp.cos(ang)] * 2, axis=-1)
    sin = jnp.concatenate([jnp.sin(ang)] * 2, axis=-1)
    n = pos.shape[0]
    pad = jnp.zeros((n, LANES - MLA_NOPE - MLA_ROPE), F32)
    q_scale = MLA_SCALE * LOG2E
    qcos = jnp.concatenate([jnp.ones((n, MLA_NOPE), F32), cos, pad], axis=-1) * q_scale
    qsin = jnp.concatenate([jnp.zeros((n, MLA_NOPE), F32), sin, pad], axis=-1) * q_scale
    kpad = jnp.zeros((n, LANES - MLA_ROPE), F32)
    kcos = jnp.concatenate([cos, kpad], axis=-1)
    ksin = jnp.concatenate([sin, kpad], axis=-1)
    return qcos, qsin, kcos, ksin


def _rotary_partner(w):
    half = MLA_ROPE // 2
    return jnp.concatenate([-w[..., half:], w[..., :half]], axis=-1)


def _mla_weights(wuq, wdkv, wuk):
    r = wuq.shape[0]
    wq3 = wuq.reshape(r, HEADS, MLA_NOPE + MLA_ROPE)
    zpad = jnp.zeros((r, HEADS, LANES - MLA_NOPE - MLA_ROPE), wuq.dtype)
    plain = jnp.concatenate([wq3, zpad], axis=-1).reshape(r, HEADS * LANES)
    partner = jnp.concatenate([jnp.zeros((r, HEADS, MLA_NOPE), wuq.dtype),
                               _rotary_partner(wq3[..., MLA_NOPE:]), zpad], axis=-1)
    wq = jnp.concatenate([plain, partner.reshape(r, HEADS * LANES)], axis=-1).astype(BF16)
    rope_w = wdkv[:, MLA_KV_RANK:]
    kz = jnp.zeros((D, LANES - MLA_ROPE), wdkv.dtype)
    wkv = jnp.concatenate([wdkv[:, :MLA_KV_RANK], rope_w, kz, _rotary_partner(rope_w), kz],
                          axis=-1).astype(BF16)
    top = jnp.concatenate([wuk.reshape(MLA_KV_RANK, HEADS, MLA_NOPE),
                           jnp.zeros((MLA_KV_RANK, HEADS, LANES - MLA_NOPE), wuk.dtype)], axis=-1)
    place = jnp.concatenate([jnp.zeros((MLA_ROPE, MLA_NOPE), F32), jnp.eye(MLA_ROPE, dtype=F32),
                             jnp.zeros((MLA_ROPE, LANES - MLA_NOPE - MLA_ROPE), F32)], axis=-1)
    place = jnp.concatenate([place, jnp.zeros((LANES - MLA_ROPE, LANES), F32)], axis=0)
    bottom = jnp.broadcast_to(place[:, None, :], (LANES, HEADS, LANES))
    wexp = jnp.concatenate([top, bottom], axis=0).reshape(2 * LANES, HEADS * LANES).astype(BF16)
    return wq, wkv, wexp


def _mla_proj(x, g, wdq, gq, wq, wkv, gkv, tables, seq):
    rows = x.shape[0]
    tm = _row_tile(rows)
    if seq >= tm:
        n_pos_tiles = seq // tm
        pos_map = lambda i: (i % n_pos_tiles, 0)
    else:
        tables = [jnp.tile(t, (tm // seq, 1)) for t in tables]
        pos_map = lambda i: (0, 0)
    row = lambda i: (i, 0)
    return pl.pallas_call(
        _mla_proj_body,
        grid=(rows // tm,),
        in_specs=[pl.BlockSpec((tm, D), row), _resident((1, D)), _resident((D, MLA_Q_RANK)),
                  _resident((1, MLA_Q_RANK)), _resident((MLA_Q_RANK, 2 * HEADS * LANES)),
                  _resident((D, 3 * LANES)), _resident((1, MLA_KV_RANK))]
                 + [pl.BlockSpec((tm, LANES), pos_map)] * 4,
        out_specs=[pl.BlockSpec((tm, HEADS * LANES), row), pl.BlockSpec((tm, LANES), row),
                   pl.BlockSpec((tm, LANES), row)],
        out_shape=[jax.ShapeDtypeStruct((rows, HEADS * LANES), BF16),
                   jax.ShapeDtypeStruct((rows, MLA_KV_RANK), F32),
                   jax.ShapeDtypeStruct((rows, LANES), F32)],
        compiler_params=_params(("parallel",)),
        name="mla_proj",
    )(x, g.reshape(1, D), wdq, gq.reshape(1, MLA_Q_RANK), wq, wkv, gkv.reshape(1, MLA_KV_RANK), *tables)


def _mla_expand_body(ckv_ref, kpe_ref, wexp_ref, wuv_ref, k_ref, v_ref):
    c = ckv_ref[...].astype(BF16)
    cat = jnp.concatenate([c, kpe_ref[...].astype(BF16)], axis=-1)
    k_ref[...] = _dot(cat, wexp_ref[...]).astype(BF16)
    v_ref[...] = _dot(c, wuv_ref[...]).astype(BF16)


def _mla_expand(ckv, kpe, wexp, wuv):
    rows = ckv.shape[0]
    tm = _row_tile(rows)
    row = lambda i: (i, 0)
    return pl.pallas_call(
        _mla_expand_body,
        grid=(rows // tm,),
        in_specs=[pl.BlockSpec((tm, LANES), row), pl.BlockSpec((tm, LANES), row),
                  _resident((2 * LANES, HEADS * LANES)), _resident((MLA_KV_RANK, D))],
        out_specs=[pl.BlockSpec((tm, HEADS * LANES), row), pl.BlockSpec((tm, D), row)],
        out_shape=[jax.ShapeDtypeStruct((rows, HEADS * LANES), BF16),
                   jax.ShapeDtypeStruct((rows, D), BF16)],
        compiler_params=_params(("parallel",)),
        name="mla_expand",
    )(ckv, kpe, wexp, wuv)


CONV_ROWS = 32
LANE_COLS = D // LANES


def _conv_body(x_ref, g_ref, w1_ref, b1_ref, hist_ref, dw_ref, bdw_ref, lng_ref, lnb_ref,
               a_ref, st_ref, gbuf, ybuf, *, tm):
    @pl.when(pl.program_id(1) == 0)
    def _():
        for c in range(LANE_COLS):
            gbuf[c, 0:HIST_PAD, :] = hist_ref[0, :, LANES * c:LANES * (c + 1)]

    h = _rms(x_ref[0], g_ref[...]).astype(BF16)
    u = _dot(h, w1_ref[...]) + b1_ref[...]
    glu = u[:, :D] * _sigmoid(u[:, D:])
    for c in range(LANE_COLS):
        gbuf[c, HIST_PAD:HIST_PAD + tm, :] = glu[:, LANES * c:LANES * (c + 1)]

    first = HIST_PAD - CONV_HIST

    def rows(i, _):
        r0 = pl.multiple_of(i * CONV_ROWS, CONV_ROWS)
        for c in range(LANE_COLS):
            cols = slice(LANES * c, LANES * (c + 1))
            acc = jnp.broadcast_to(bdw_ref[:, cols], (CONV_ROWS, LANES))
            for w in range(CONV_WIDTH):
                acc = acc + dw_ref[w:w + 1, cols] * gbuf[c, pl.ds(r0 + first + w, CONV_ROWS), :]
            ybuf[pl.ds(r0, CONV_ROWS), cols] = acc
        return 0

    lax.fori_loop(0, tm // CONV_ROWS, rows, 0)

    y = ybuf[...]
    yc = y - jnp.mean(y, axis=-1, keepdims=True)
    var = jnp.mean(yc * yc, axis=-1, keepdims=True)
    yn = yc * lax.rsqrt(var + EPS) * lng_ref[...] + lnb_ref[...]
    a_ref[0] = (yn * _sigmoid(yn)).astype(BF16)

    for c in range(LANE_COLS):
        tail = gbuf[c, tm:tm + HIST_PAD, :]
        st_ref[0, :, LANES * c:LANES * (c + 1)] = tail
        gbuf[c, 0:HIST_PAD, :] = tail


def _conv(x, g, w1, b1, hist, dw, bdw, lng, lnb):
    b, s, _ = x.shape
    tm = _row_tile(s)
    hist = jnp.pad(hist.astype(F32), ((0, 0), (HIST_PAD - CONV_HIST, 0), (0, 0)))
    dw = jnp.pad(dw.astype(F32), ((0, HIST_PAD - CONV_WIDTH), (0, 0)))
    a, st = pl.pallas_call(
        functools.partial(_conv_body, tm=tm),
        grid=(b, s // tm),
        in_specs=[pl.BlockSpec((1, tm, D), lambda bi, t: (bi, t, 0)), _resident((1, D)),
                  _resident((D, 2 * D)), _resident((1, 2 * D)),
                  pl.BlockSpec((1, HIST_PAD, D), lambda bi, t: (bi, 0, 0)),
                  _resident((HIST_PAD, D)), _resident((1, D)), _resident((1, D)), _resident((1, D))],
        out_specs=[pl.BlockSpec((1, tm, D), lambda bi, t: (bi, t, 0)),
                   pl.BlockSpec((1, HIST_PAD, D), lambda bi, t: (bi, 0, 0))],
        out_shape=[jax.ShapeDtypeStruct((b, s, D), BF16),
                   jax.ShapeDtypeStruct((b, HIST_PAD, D), F32)],
        scratch_shapes=[pltpu.VMEM((LANE_COLS, HIST_PAD + tm, LANES), F32), pltpu.VMEM((tm, D), F32)],
        compiler_params=_params(("parallel", "arbitrary")),
        name="conv_module",
    )(x, g.reshape(1, D), w1, b1.reshape(1, 2 * D), hist, dw, bdw.reshape(1, D),
      lng.reshape(1, D), lnb.reshape(1, D))
    return a, st[:, HIST_PAD - CONV_HIST:]


def _post_body(x_ref, a_ref, p_ref, wo_ref, bo_ref, gf_ref, w13_ref, w2_ref, gp_ref, wg_ref, wp_ref,
               gfin_ref, o_ref, h_s, acc_s, *, final):
    x1 = x_ref[...] + _dot(a_ref[...], wo_ref[...]) + bo_ref[...]
    h_s[...] = _rms(x1, gf_ref[...]).astype(BF16)
    acc_s[...] = x1

    def gated(c):
        uv = _dot(h_s[...], w13_ref[c])
        u = uv[:, :FFN_CHUNK]
        return _dot((u * _sigmoid(u) * uv[:, FFN_CHUNK:]).astype(BF16), w2_ref[c])

    def chunk_pair(i, _):
        acc_s[...] += gated(2 * i) + gated(2 * i + 1)
        return 0

    lax.fori_loop(0, N_FFN_CHUNKS // 2, chunk_pair, 0)
    for c in range(N_FFN_CHUNKS // 2 * 2, N_FFN_CHUNKS):
        acc_s[...] += gated(c)
    x2 = acc_s[...]
    gate = _sigmoid(_dot(_rms(x2, gp_ref[...]).astype(BF16), wg_ref[...]))
    x3 = x2 + gate * _dot(p_ref[...].astype(BF16), wp_ref[...])
    o_ref[...] = _rms(x3, gfin_ref[...]) if final else x3


def _post(x, a, p, wo, bo, gf, w13, w2, gp, wg, wp, gfin, *, final):
    rows = x.shape[0]
    tm = _row_tile(rows)
    row = lambda i: (i, 0)
    return pl.pallas_call(
        functools.partial(_post_body, final=final),
        grid=(rows // tm,),
        in_specs=[pl.BlockSpec((tm, D), row), pl.BlockSpec((tm, D), row), pl.BlockSpec((tm, PLE_DIM), row),
                  _resident((D, D)), _resident((1, D)), _resident((1, D)),
                  _resident((N_FFN_CHUNKS, D, 2 * FFN_CHUNK)), _resident((N_FFN_CHUNKS, FFN_CHUNK, D)),
                  _resident((1, D)), _resident((D, D)), _resident((PLE_DIM, D)), _resident((1, D))],
        out_specs=pl.BlockSpec((tm, D), row),
        out_shape=jax.ShapeDtypeStruct((rows, D), F32),
        scratch_shapes=[pltpu.VMEM((tm, D), BF16), pltpu.VMEM((tm, D), F32)],
        compiler_params=_params(("parallel",)),
        name="post_mixer",
    )(x, a, p, wo, bo.reshape(1, D), gf.reshape(1, D), w13, w2, gp.reshape(1, D), wg, wp,
      gfin.reshape(1, D))


def _ffn_weights(w1, w3, w2):
    split = lambda w: w.reshape(D, N_FFN_CHUNKS, FFN_CHUNK).transpose(1, 0, 2)
    w13 = jnp.concatenate([split(w1), split(w3)], axis=-1).astype(BF16)
    return w13, w2.reshape(N_FFN_CHUNKS, FFN_CHUNK, D).astype(BF16)


def _with_cache(cache, new, total):
    b, past, w = cache.shape
    pad = jnp.zeros((b, total - past - new.shape[1], w), new.dtype)
    return jnp.concatenate([cache.astype(new.dtype), new, pad], axis=1)


def kernel(x_prompt, x_sample, p_prompt, p_sample, cache_sb_k, cache_sb_v, cache_mla_ckv, cache_mla_kpe, state_conv, cache_band_k, cache_band_v, norm_mix, norm_ffn, norm_ple, norm_final, ffn_w1, ffn_w3, ffn_w2, ple_proj, ple_gate, sb_wqkv, sb_wo, mla_wdq, mla_gq, mla_wuq, mla_wdkv, mla_gkv, mla_wuk, mla_wuv, mla_wo, conv_w1, conv_b1, conv_dw, conv_bdw, conv_ln_g, conv_ln_b, conv_w2, conv_b2, band_wqkv, band_rel_bias, band_wo):
    bp, s, _ = x_prompt.shape
    bs, t, _ = x_sample.shape
    depth = norm_mix.shape[0]
    past = cache_sb_k.shape[2]
    xp = x_prompt.reshape(bp * s, D)
    xs = x_sample.reshape(bs * t, D)
    pos_p = jnp.arange(s, dtype=jnp.int32)
    pos_s = past + jnp.arange(t, dtype=jnp.int32)
    zero_bias = jnp.zeros((D,), F32)
    head_scale = HEAD_DIM ** -0.5 * LOG2E
    sample_keys = -(-(past + t) // KEY_TILE) * KEY_TILE
    mla_keys = -(-(past + t) // SM_KEY_TILE) * SM_KEY_TILE

    outs = {name: [] for name in ("sb_kp", "sb_vp", "sb_ks", "sb_vs", "mla_cp", "mla_pp", "mla_cs", "mla_ps",
                                  "conv_p", "conv_s", "band_kp", "band_vp", "band_ks", "band_vs")}

    for i in range(depth):
        m, j = i % N_MIXERS, i // N_MIXERS
        if m == 0:
            w = sb_wqkv[j].astype(BF16)
            q, kf, vf, kb, vb = _qkv(xp, norm_mix[i], w, head_scale)
            ap = _sb_attn(q.reshape(bp, s, D), kb.reshape(bp, s, D), vb.reshape(bp, s, D),
                          tq=ROW_TILE, q_off=0)
            outs["sb_kp"].append(kf.reshape(bp, s, HEADS, HEAD_DIM))
            outs["sb_vp"].append(vf.reshape(bp, s, HEADS, HEAD_DIM))
            q, kf, vf, kb, vb = _qkv(xs, norm_mix[i], w, head_scale)
            k_all = _with_cache(cache_sb_k[j].reshape(bs, past, D), kb.reshape(bs, t, D), sample_keys)
            v_all = _with_cache(cache_sb_v[j].reshape(bs, past, D), vb.reshape(bs, t, D), sample_keys)
            as_ = _sb_attn(q.reshape(bs, t, D), k_all, v_all, tq=t, q_off=past // t)
            outs["sb_ks"].append(kf.reshape(bs, t, HEADS, HEAD_DIM))
            outs["sb_vs"].append(vf.reshape(bs, t, HEADS, HEAD_DIM))
            wo, bo = sb_wo[j].astype(BF16), zero_bias
        elif m == 1:
            wq, wkv, wexp = _mla_weights(mla_wuq[j], mla_wdkv[j], mla_wuk[j])
            wdq, wuv = mla_wdq[j].astype(BF16), mla_wuv[j].astype(BF16)
            q, ckv, kpe = _mla_proj(xp, norm_mix[i], wdq, mla_gq[j], wq, wkv, mla_gkv[j],
                                    _rope_tables(pos_p), s)
            k, v = _mla_expand(ckv, kpe, wexp, wuv)
            ap = _softmax_attn(q.reshape(bp, s, -1), k.reshape(bp, s, -1), v.reshape(bp, s, D),
                               tq=ROW_TILE, q_off=0)
            outs["mla_cp"].append(ckv.reshape(bp, s, MLA_KV_RANK))
            outs["mla_pp"].append(kpe[:, :MLA_ROPE].reshape(bp, s, MLA_ROPE))
            q, ckv, kpe = _mla_proj(xs, norm_mix[i], wdq, mla_gq[j], wq, wkv, mla_gkv[j],
                                    _rope_tables(pos_s), t)
            ckv_all = _with_cache(cache_mla_ckv[j], ckv.reshape(bs, t, MLA_KV_RANK), mla_keys)
            kpe_cache = jnp.pad(cache_mla_kpe[j], ((0, 0), (0, 0), (0, LANES - MLA_ROPE)))
            kpe_all = _with_cache(kpe_cache, kpe.reshape(bs, t, LANES), mla_keys)
            k, v = _mla_expand(ckv_all.reshape(bs * mla_keys, MLA_KV_RANK),
                               kpe_all.reshape(bs * mla_keys, LANES), wexp, wuv)
            as_ = _softmax_attn(q.reshape(bs, t, -1), k.reshape(bs, mla_keys, -1),
                                v.reshape(bs, mla_keys, D), tq=t, q_off=past // t)
            outs["mla_cs"].append(ckv.reshape(bs, t, MLA_KV_RANK))
            outs["mla_ps"].append(kpe[:, :MLA_ROPE].reshape(bs, t, MLA_ROPE))
            wo, bo = mla_wo[j].astype(BF16), zero_bias
        elif m == 2:
            w1 = conv_w1[j].astype(BF16)
            ap, st = _conv(xp.reshape(bp, s, D), norm_mix[i], w1, conv_b1[j],
                           jnp.zeros((bp, CONV_HIST, D), F32), conv_dw[j], conv_bdw[j],
                           conv_ln_g[j], conv_ln_b[j])
            outs["conv_p"].append(st)
            as_, st = _conv(xs.reshape(bs, t, D), norm_mix[i], w1, conv_b1[j], state_conv[j],
                            conv_dw[j], conv_bdw[j], conv_ln_g[j], conv_ln_b[j])
            outs["conv_s"].append(st)
            wo, bo = conv_w2[j].astype(BF16), conv_b2[j]
        else:
            w = band_wqkv[j].astype(BF16)
            bias_tiles = _band_bias_layouts(band_rel_bias[j])
            q, kf, vf, kb, vb = _qkv(xp, norm_mix[i], w, head_scale)
            ap = _band_attn(q.reshape(bp, s, D), kb.reshape(bp, s, D), vb.reshape(bp, s, D),
                            bias_tiles, q_off=0)
            keep = min(BAND_PAST, s)
            outs["band_kp"].append(kf.reshape(bp, s, HEADS, HEAD_DIM)[:, s - keep:])
            outs["band_vp"].append(vf.reshape(bp, s, HEADS, HEAD_DIM)[:, s - keep:])
            q, kf, vf, kb, vb = _qkv(xs, norm_mix[i], w, head_scale)
            n_past = cache_band_k.shape[2]
            local = n_past + BAND_TQ
            k_all = _with_cache(cache_band_k[j].reshape(bs, n_past, D), kb.reshape(bs, t, D), local)
            v_all = _with_cache(cache_band_v[j].reshape(bs, n_past, D), vb.reshape(bs, t, D), local)
            q_pad = jnp.pad(q.reshape(bs, t, D), ((0, 0), (0, BAND_TQ - t), (0, 0)))
            as_ = _band_attn(q_pad, k_all, v_all, bias_tiles, q_off=n_past // BAND_TQ)[:, :t]
            kk = jnp.concatenate([cache_band_k[j], kf.reshape(bs, t, HEADS, HEAD_DIM)], axis=1)
            vv = jnp.concatenate([cache_band_v[j], vf.reshape(bs, t, HEADS, HEAD_DIM)], axis=1)
            outs["band_ks"].append(kk[:, t:])
            outs["band_vs"].append(vv[:, t:])
            wo, bo = band_wo[j].astype(BF16), zero_bias

        w13, w2 = _ffn_weights(ffn_w1[i], ffn_w3[i], ffn_w2[i])
        final = i == depth - 1
        post = functools.partial(_post, wo=wo, bo=bo, gf=norm_ffn[i], w13=w13, w2=w2, gp=norm_ple[i],
                                 wg=ple_gate[i].astype(BF16), wp=ple_proj[i].astype(BF16),
                                 gfin=norm_final, final=final)
        xp = post(xp, ap.reshape(bp * s, D), p_prompt[i].reshape(bp * s, PLE_DIM))
        xs = post(xs, as_.reshape(bs * t, D), p_sample[i].reshape(bs * t, PLE_DIM))

    stack = lambda name: jnp.stack(outs[name])
    return (xp.reshape(bp, s, D), xs.reshape(bs, t, D),
            stack("sb_kp"), stack("sb_vp"), stack("sb_ks"), stack("sb_vs"),
            stack("mla_cp"), stack("mla_pp"), stack("mla_cs"), stack("mla_ps"),
            stack("conv_p"), stack("conv_s"),
            stack("band_kp"), stack("band_vp"), stack("band_ks"), stack("band_vs"))
```

```python
import functools
import math

import jax
import jax.numpy as jnp
from jax import lax
from jax.experimental import pallas as pl
from jax.experimental.pallas import tpu as pltpu

F32 = jnp.float32
BF16 = jnp.bfloat16

D = 1024
EPS = 1e-6
NEG = -1e30
HEADS = 16
HEAD_DIM = 64
CHUNK = 64
PLE_DIM = 256
FFN_HIDDEN = 2816
FFN_CHUNK = 256
N_FFN_CHUNKS = FFN_HIDDEN // FFN_CHUNK
MLA_Q_RANK = 256
MLA_KV_RANK = 128
MLA_NOPE = 64
MLA_ROPE = 32
MLA_SCALE = (MLA_NOPE + MLA_ROPE) ** -0.5
ROPE_THETA = 10000.0
CONV_WIDTH = 31
CONV_HIST = CONV_WIDTH - 1
HIST_PAD = 32
BAND_CHUNKS = 8
BAND_PAST = BAND_CHUNKS * CHUNK
REL_CLIP = 128
N_MIXERS = 4

LANES = 128
KEY_TILE = 256
SB_DEAD_BITS = 160.0
SM_KEY_TILE = 2 * KEY_TILE
SM_HEADS = 2
SM_BLOCKS = 2
LOG2E = math.log2(math.e)
ROW_TILE = 512
VMEM_LIMIT = 56 * 1024 * 1024


def _dot(a, b):
    return jnp.dot(a, b, preferred_element_type=F32)


def _dot_nt(a, b):
    return lax.dot_general(a, b, (((1,), (1,)), ((), ())), preferred_element_type=F32)


def _rms(x, g):
    return x * lax.rsqrt(jnp.mean(x * x, axis=-1, keepdims=True) + EPS) * g


def _sigmoid(x):
    return 1.0 / (1.0 + jnp.exp(-x))


def _params(sem):
    return pltpu.CompilerParams(dimension_semantics=sem, vmem_limit_bytes=VMEM_LIMIT)


def _resident(shape):
    zeros = (0,) * len(shape)
    return pl.BlockSpec(shape, lambda *_: zeros, pipeline_mode=pl.Buffered(1))


def _row_tile(rows):
    return min(ROW_TILE, rows)


def _chunk_of(idx):
    return lax.shift_right_arithmetic(idx, CHUNK.bit_length() - 1)


def _head_lanes(lane, h):
    return lane < HEAD_DIM if h == 0 else lane >= HEAD_DIM


def _rows_by_head(rows):
    lane = lax.broadcasted_iota(jnp.int32, rows.shape, 1)
    zero = jnp.zeros_like(rows)
    return jnp.concatenate([jnp.where(_head_lanes(lane, h), rows, zero) for h in range(2)], axis=0)


def _qkv_body(x_ref, g_ref, w_ref, q_ref, kf_ref, vf_ref, kb_ref, vb_ref, *, scale):
    h = _rms(x_ref[...], g_ref[...]).astype(BF16)
    q_ref[...] = (_dot(h, w_ref[:, 0:D]) * scale).astype(BF16)
    k = _dot(h, w_ref[:, D:2 * D])
    kf_ref[...] = k.reshape(kf_ref.shape)
    kb_ref[...] = k.astype(BF16)
    v = _dot(h, w_ref[:, 2 * D:3 * D])
    vf_ref[...] = v.reshape(vf_ref.shape)
    vb_ref[...] = v.astype(BF16)


def _qkv(x, g, w, scale):
    rows = x.shape[0]
    tm = _row_tile(rows)
    row = lambda i: (i, 0)
    out_sds = lambda dt: jax.ShapeDtypeStruct((rows, D), dt)
    heads_spec = pl.BlockSpec((tm, HEADS, HEAD_DIM), lambda i: (i, 0, 0))
    heads_sds = jax.ShapeDtypeStruct((rows, HEADS, HEAD_DIM), F32)
    return pl.pallas_call(
        functools.partial(_qkv_body, scale=scale),
        grid=(rows // tm,),
        in_specs=[pl.BlockSpec((tm, D), row), _resident((1, D)), _resident((D, 3 * D))],
        out_specs=[pl.BlockSpec((tm, D), row), heads_spec, heads_spec,
                   pl.BlockSpec((tm, D), row), pl.BlockSpec((tm, D), row)],
        out_shape=[out_sds(BF16), heads_sds, heads_sds, out_sds(BF16), out_sds(BF16)],
        compiler_params=_params(("parallel",)),
        name="qkv_proj",
    )(x, g.reshape(1, D), w)


def _sb_attn_body(q_ref, k_ref, v_ref, tri_ref, o_ref, *, tq, q_off):
    q0 = (pl.program_id(2) + q_off) * tq
    rows = min(tq, KEY_TILE)
    lane = lax.broadcasted_iota(jnp.int32, (rows, LANES), 1)
    diff = (lax.broadcasted_iota(jnp.int32, (rows, KEY_TILE), 1)
            - lax.broadcasted_iota(jnp.int32, (rows, KEY_TILE), 0))
    col_minus_row = jnp.concatenate([diff, diff], axis=1)
    tri = tri_ref[...]

    def tile(r0, j, masked):
        k0 = pl.multiple_of(j * KEY_TILE, KEY_TILE)
        z = _dot_nt(q_ref[0, r0:r0 + rows, :], _rows_by_head(k_ref[0, pl.ds(k0, KEY_TILE), :]))
        neg_abs = pltpu.bitcast(pltpu.bitcast(z, jnp.uint32) | jnp.uint32(0x80000000), F32)
        drop = jnp.maximum(z, 0.0) + jnp.log(1.0 + jnp.exp2(neg_abs)) * LOG2E
        if masked:
            valid = col_minus_row < (q0 + r0 - k0)
            drop = jnp.where(valid, drop, 0.0)
        hi = drop.astype(BF16)
        later = jnp.concatenate([_dot(hi[:, :KEY_TILE], tri), _dot(hi[:, KEY_TILE:], tri)], axis=1)
        w = jnp.exp2((z - drop) - later)
        if masked:
            w = jnp.where(valid, w, 0.0)
        o = _dot(w.astype(BF16), _rows_by_head(v_ref[0, pl.ds(k0, KEY_TILE), :]))
        total = jnp.where(lane < HEAD_DIM, jnp.sum(drop[:, :KEY_TILE], axis=-1, keepdims=True),
                          jnp.sum(drop[:, KEY_TILE:], axis=-1, keepdims=True))
        return o, total

    def add_tile(r0, j, carry, masked):
        dropped, out = carry
        o, total = tile(r0, j, masked)
        return dropped + total, out + jnp.exp2(-dropped) * o

    def alive(carry):
        return (jnp.min(carry[0]) < SB_DEAD_BITS).astype(jnp.int32)

    for r0 in range(0, tq, rows):
        own = lax.div(q0 + r0, KEY_TILE)
        carry = add_tile(r0, own, (jnp.zeros((rows, LANES), F32), jnp.zeros((rows, LANES), F32)), True)

        def more(state, own=own):
            t, live, _ = state
            return jnp.logical_and(t < own, live > 0)

        def farther(state, own=own, r0=r0):
            t, _, carry = state
            carry = add_tile(r0, own - 1 - t, carry, False)
            return t + 1, alive(carry), carry

        carry = lax.while_loop(more, farther, (jnp.int32(0), alive(carry), carry))[2]
        o_ref[0, r0:r0 + rows, :] = carry[1].astype(BF16)


def _tri():
    j = lax.broadcasted_iota(jnp.int32, (KEY_TILE, KEY_TILE), 0)
    s = lax.broadcasted_iota(jnp.int32, (KEY_TILE, KEY_TILE), 1)
    return (j > s).astype(BF16)


def _sb_attn(q, k, v, *, tq, q_off):
    b, sq, _ = q.shape
    sk = k.shape[1]
    rows = min(tq, KEY_TILE)
    assert tq % rows == 0 and KEY_TILE % rows == 0 and (q_off * tq) % rows == 0
    assert -(-((q_off + sq // tq) * tq) // KEY_TILE) * KEY_TILE <= sk
    return pl.pallas_call(
        functools.partial(_sb_attn_body, tq=tq, q_off=q_off),
        grid=(b, HEADS // 2, sq // tq),
        in_specs=[pl.BlockSpec((1, tq, LANES), lambda bi, p, qi: (bi, qi, p)),
                  pl.BlockSpec((1, sk, LANES), lambda bi, p, qi: (bi, 0, p)),
                  pl.BlockSpec((1, sk, LANES), lambda bi, p, qi: (bi, 0, p)),
                  pl.BlockSpec((KEY_TILE, KEY_TILE), lambda bi, p, qi: (0, 0))],
        out_specs=pl.BlockSpec((1, tq, LANES), lambda bi, p, qi: (bi, qi, p)),
        out_shape=jax.ShapeDtypeStruct((b, sq, D), BF16),
        compiler_params=_params(("parallel", "parallel", "arbitrary")),
        name="sb_attn",
    )(q, k, v, _tri())


def _value_with_ones(v_tile, h):
    lane = lax.broadcasted_iota(jnp.int32, v_tile.shape, 1)
    return jnp.where(_head_lanes(lane, h), v_tile, jnp.ones_like(v_tile))


def _normalise_pair(accs):
    res = [a * pltpu.roll(1.0 / a, HEAD_DIM, 1) for a in accs]
    lane = lax.broadcasted_iota(jnp.int32, res[0].shape, 1)
    return jnp.where(lane < HEAD_DIM, res[0], res[1])


def _softmax_attn_body(q_ref, k_ref, v_ref, o_ref, *, tq, q_off, blocks):
    first = pl.program_id(2) * blocks + q_off
    chunk_diff = (_chunk_of(lax.broadcasted_iota(jnp.int32, (tq, SM_KEY_TILE), 1))
                  - _chunk_of(lax.broadcasted_iota(jnp.int32, (tq, SM_KEY_TILE), 0)))

    def chain(r0, n, h, j, state, visible):
        m, acc = state
        k0 = pl.multiple_of(j * SM_KEY_TILE, SM_KEY_TILE)
        s = _dot_nt(q_ref[0, r0:r0 + n, LANES * h:LANES * (h + 1)],
                    k_ref[0, pl.ds(k0, SM_KEY_TILE), LANES * h:LANES * (h + 1)])
        if visible is not None:
            s = jnp.where(visible, s, NEG)
        m_new = jnp.maximum(m, jnp.max(s, axis=-1, keepdims=True))
        p = jnp.exp2(s - m_new)
        v_pair = v_ref[0, pl.ds(k0, SM_KEY_TILE), LANES * (h // 2):LANES * (h // 2 + 1)]
        return m_new, acc * jnp.exp2(m - m_new) + _dot(p.astype(BF16), _value_with_ones(v_pair, h % 2))

    def full_tiles(n, tiles, carry):
        new = list(carry)
        for j in tiles:
            for h in range(SM_HEADS):
                new[h] = chain(0, n, h, j, new[h], None)
        return tuple(new)

    n = blocks * tq
    carry = tuple((jnp.full((n, 1), NEG, F32), jnp.zeros((n, LANES), F32)) for _ in range(SM_HEADS))
    if blocks == 1:
        last = lax.div(first * tq, SM_KEY_TILE)
        pairs = lax.div(last, 2)
        carry = lax.fori_loop(0, pairs, lambda t, c: full_tiles(n, [2 * t, 2 * t + 1], c), carry)
        carry = lax.fori_loop(2 * pairs, last, lambda t, c: full_tiles(n, [t], c), carry)
        own = chunk_diff <= _chunk_of(first * tq - last * SM_KEY_TILE)
        carry = tuple(chain(0, n, h, last, carry[h], own) for h in range(SM_HEADS))
    else:
        carry = lax.fori_loop(0, lax.div(first, 2), lambda t, c: full_tiles(n, [2 * t, 2 * t + 1], c), carry)
        for i in range(blocks):
            r0 = i * tq
            later_blocks = [jnp.full_like(chunk_diff, -1)] * (blocks - 1 - i)
            visible = jnp.concatenate([chunk_diff] + later_blocks, axis=0) <= 0
            stacked = []
            for h in range(SM_HEADS):
                m, acc = carry[h]
                m1, acc1 = chain(r0, n - r0, h, first + i, (m[r0:], acc[r0:]), visible)
                if r0:
                    m1, acc1 = jnp.concatenate([m[:r0], m1], axis=0), jnp.concatenate([acc[:r0], acc1], axis=0)
                stacked.append((m1, acc1))
            carry = tuple(stacked)
    for pair in range(SM_HEADS // 2):
        o_ref[0, :, LANES * pair:LANES * (pair + 1)] = _normalise_pair(
            [carry[2 * pair][1], carry[2 * pair + 1][1]]).astype(BF16)


def _softmax_attn(q, k, v, *, tq, q_off):
    b, sq, _ = q.shape
    sk = k.shape[1]
    assert tq <= SM_KEY_TILE and SM_KEY_TILE % tq == 0 and sk % SM_KEY_TILE == 0
    paired = tq == SM_KEY_TILE and (sq // tq) % SM_BLOCKS == 0 and q_off % 2 == 0
    blocks = SM_BLOCKS if paired else 1
    qk_lanes, v_lanes = SM_HEADS * LANES, SM_HEADS * HEAD_DIM
    return pl.pallas_call(
        functools.partial(_softmax_attn_body, tq=tq, q_off=q_off, blocks=blocks),
        grid=(b, HEADS // SM_HEADS, sq // (tq * blocks)),
        in_specs=[pl.BlockSpec((1, tq * blocks, qk_lanes), lambda bi, p, qi: (bi, qi, p)),
                  pl.BlockSpec((1, sk, qk_lanes), lambda bi, p, qi: (bi, 0, p)),
                  pl.BlockSpec((1, sk, v_lanes), lambda bi, p, qi: (bi, 0, p))],
        out_specs=pl.BlockSpec((1, tq * blocks, v_lanes), lambda bi, p, qi: (bi, qi, p)),
        out_shape=jax.ShapeDtypeStruct((b, sq, D), BF16),
        compiler_params=_params(("parallel", "parallel", "arbitrary")),
        name="mla_attn",
    )(q, k, v)


BAND_TQ = 4 * CHUNK


BAND_KEYS = 3 * KEY_TILE
BAND_BLOCKS = 8


def _band_attn_body(q_ref, k_ref, v_ref, bias_ref, o_ref, *, q_off, blocks):
    row = lax.broadcasted_iota(jnp.int32, (2 * BAND_KEYS, LANES), 0)
    lane = lax.broadcasted_iota(jnp.int32, (2 * BAND_KEYS, LANES), 1)
    ones = jnp.where((row < BAND_KEYS) == (lane < HEAD_DIM), 1.0, 0.0).astype(BF16)
    for g in range(blocks):
        rows = slice(BAND_TQ * g, BAND_TQ * (g + 1))
        qi = pl.program_id(2) * blocks + g + q_off
        k0 = pl.multiple_of(jnp.maximum(qi - 2, 0) * KEY_TILE, KEY_TILE)
        k_win = k_ref[0, pl.ds(k0, BAND_KEYS), :]
        v_win = v_ref[0, pl.ds(k0, BAND_KEYS), :]
        s = _dot_nt(q_ref[0, rows, :], _rows_by_head(k_win)) + bias_ref[0, jnp.minimum(qi, 2)]
        p = jnp.concatenate(
            [jnp.exp2(sh - jnp.max(sh, axis=-1, keepdims=True)) for sh in (s[:, :BAND_KEYS], s[:, BAND_KEYS:])],
            axis=1).astype(BF16)
        acc = _dot(p, jnp.concatenate([_rows_by_head(v_win), ones], axis=1))
        o_ref[0, rows, :] = (acc[:, :LANES] * (1.0 / acc[:, LANES:])).astype(BF16)


def _band_bias_tiles(rel_bias):
    table = rel_bias.astype(F32) * LOG2E
    heads = table.shape[0]
    n = BAND_TQ + KEY_TILE - 1
    tiles = []
    for t in range(3):
        d_lo = BAND_PAST - KEY_TILE * t - (KEY_TILE - 1)
        below = max(0, min(n, -REL_CLIP - d_lo))
        above = max(0, min(n, d_lo + n - 1 - REL_CLIP))
        mid_lo = d_lo + below + REL_CLIP
        diag = jnp.concatenate([jnp.broadcast_to(table[:, :1], (heads, below)),
                                table[:, mid_lo:mid_lo + n - below - above],
                                jnp.broadcast_to(table[:, -1:], (heads, above))], axis=1)
        rows = jnp.tile(diag, (1, BAND_TQ + 1))[:, :BAND_TQ * (n + 1)].reshape(heads, BAND_TQ, n + 1)
        tiles.append(rows[:, :, KEY_TILE - 1::-1])
    return jnp.stack(tiles, axis=1)


def _band_bias_layouts(rel_bias):
    tiles = _band_bias_tiles(rel_bias)
    heads = tiles.shape[0]
    row_chunk = (jnp.arange(BAND_TQ, dtype=jnp.int32) // CHUNK)[:, None]
    col_chunk = (jnp.arange(KEY_TILE, dtype=jnp.int32) // CHUNK)[None, :]
    eight_back = jnp.where(row_chunk <= col_chunk, tiles[:, 0], NEG)
    four_back = tiles[:, 1]
    own = jnp.where(col_chunk <= row_chunk, tiles[:, 2], NEG)
    ahead = jnp.full_like(own, NEG)
    layouts = [jnp.concatenate(slots, axis=-1) for slots in
               ((own, ahead, ahead), (four_back, own, ahead), (eight_back, four_back, own))]
    per_head = jnp.stack(layouts, axis=1)
    pairs = per_head.reshape(heads // 2, 2, 3, BAND_TQ, BAND_KEYS)
    return jnp.concatenate([pairs[:, 0], pairs[:, 1]], axis=-1)


def _band_attn(q, k, v, bias_layouts, *, q_off):
    b, sq, _ = q.shape
    sk = k.shape[1]
    assert sk >= BAND_KEYS and (q_off + sq // BAND_TQ) * BAND_TQ <= sk
    blocks = math.gcd(sq // BAND_TQ, BAND_BLOCKS)
    rows = blocks * BAND_TQ
    return pl.pallas_call(
        functools.partial(_band_attn_body, q_off=q_off, blocks=blocks),
        grid=(HEADS // 2, b, sq // rows),
        in_specs=[pl.BlockSpec((1, rows, LANES), lambda p, bi, qi: (bi, qi, p)),
                  pl.BlockSpec((1, sk, LANES), lambda p, bi, qi: (bi, 0, p)),
                  pl.BlockSpec((1, sk, LANES), lambda p, bi, qi: (bi, 0, p)),
                  pl.BlockSpec((1, 3, BAND_TQ, 2 * BAND_KEYS), lambda p, bi, qi: (p, 0, 0, 0))],
        out_specs=pl.BlockSpec((1, rows, LANES), lambda p, bi, qi: (bi, qi, p)),
        out_shape=jax.ShapeDtypeStruct((b, sq, D), BF16),
        compiler_params=_params(("parallel", "parallel", "arbitrary")),
        name="band_attn",
    )(q, k, v, bias_layouts)


def _mla_proj_body(x_ref, g_ref, wdq_ref, gq_ref, wq_ref, wkv_ref, gkv_ref,
                   qcos_ref, qsin_ref, kcos_ref, ksin_ref, q_ref, ckv_ref, kpe_ref):
    h = _rms(x_ref[...], g_ref[...]).astype(BF16)
    cq = _rms(_dot(h, wdq_ref[...]), gq_ref[...]).astype(BF16)
    qq = _dot(cq, wq_ref[...])
    qcos, qsin = qcos_ref[...], qsin_ref[...]
    half = HEADS * LANES
    for hd in range(HEADS):
        lo = LANES * hd
        q_ref[:, lo:lo + LANES] = (qq[:, lo:lo + LANES] * qcos
                                   + qq[:, half + lo:half + lo + LANES] * qsin).astype(BF16)
    kv = _dot(h, wkv_ref[...])
    ckv_ref[...] = _rms(kv[:, 0:LANES], gkv_ref[...])
    kpe_ref[...] = kv[:, LANES:2 * LANES] * kcos_ref[...] + kv[:, 2 * LANES:3 * LANES] * ksin_ref[...]


def _rope_tables(pos):
    half = MLA_ROPE // 2
    inv = ROPE_THETA ** (-jnp.arange(half, dtype=F32) / half)
    ang = pos.astype(F32)[:, None] * inv[None, :]
    cos = jnp.concatenate([jnp.cos(ang)] * 2, axis=-1)
    sin = jnp.concatenate([jnp.sin(ang)] * 2, axis=-1)
    n = pos.shape[0]
    pad = jnp.zeros((n, LANES - MLA_NOPE - MLA_ROPE), F32)
    q_scale = MLA_SCALE * LOG2E
    qcos = jnp.concatenate([jnp.ones((n, MLA_NOPE), F32), cos, pad], axis=-1) * q_scale
    qsin = jnp.concatenate([jnp.zeros((n, MLA_NOPE), F32), sin, pad], axis=-1) * q_scale
    kpad = jnp.zeros((n, LANES - MLA_ROPE), F32)
    kcos = jnp.concatenate([cos, kpad], axis=-1)
    ksin = jnp.concatenate([sin, kpad], axis=-1)
    return qcos, qsin, kcos, ksin


def _rotary_partner(w):
    half = MLA_ROPE // 2
    return jnp.concatenate([-w[..., half:], w[..., :half]], axis=-1)


def _mla_weights(wuq, wdkv, wuk):
    r = wuq.shape[0]
    wq3 = wuq.reshape(r, HEADS, MLA_NOPE + MLA_ROPE)
    zpad = jnp.zeros((r, HEADS, LANES - MLA_NOPE - MLA_ROPE), wuq.dtype)
    plain = jnp.concatenate([wq3, zpad], axis=-1).reshape(r, HEADS * LANES)
    partner = jnp.concatenate([jnp.zeros((r, HEADS, MLA_NOPE), wuq.dtype),
                               _rotary_partner(wq3[..., MLA_NOPE:]), zpad], axis=-1)
    wq = jnp.concatenate([plain, partner.reshape(r, HEADS * LANES)], axis=-1).astype(BF16)
    rope_w = wdkv[:, MLA_KV_RANK:]
    kz = jnp.zeros((D, LANES - MLA_ROPE), wdkv.dtype)
    wkv = jnp.concatenate([wdkv[:, :MLA_KV_RANK], rope_w, kz, _rotary_partner(rope_w), kz],
                          axis=-1).astype(BF16)
    top = jnp.concatenate([wuk.reshape(MLA_KV_RANK, HEADS, MLA_NOPE),
                           jnp.zeros((MLA_KV_RANK, HEADS, LANES - MLA_NOPE), wuk.dtype)], axis=-1)
    place = jnp.concatenate([jnp.zeros((MLA_ROPE, MLA_NOPE), F32), jnp.eye(MLA_ROPE, dtype=F32),
                             jnp.zeros((MLA_ROPE, LANES - MLA_NOPE - MLA_ROPE), F32)], axis=-1)
    place = jnp.concatenate([place, jnp.zeros((LANES - MLA_ROPE, LANES), F32)], axis=0)
    bottom = jnp.broadcast_to(place[:, None, :], (LANES, HEADS, LANES))
    wexp = jnp.concatenate([top, bottom], axis=0).reshape(2 * LANES, HEADS * LANES).astype(BF16)
    return wq, wkv, wexp


def _mla_proj(x, g, wdq, gq, wq, wkv, gkv, tables, seq):
    rows = x.shape[0]
    tm = _row_tile(rows)
    if seq >= tm:
        n_pos_tiles = seq // tm
        pos_map = lambda i: (i % n_pos_tiles, 0)
    else:
        tables = [jnp.tile(t, (tm // seq, 1)) for t in tables]
        pos_map = lambda i: (0, 0)
    row = lambda i: (i, 0)
    return pl.pallas_call(
        _mla_proj_body,
        grid=(rows // tm,),
        in_specs=[pl.BlockSpec((tm, D), row), _resident((1, D)), _resident((D, MLA_Q_RANK)),
                  _resident((1, MLA_Q_RANK)), _resident((MLA_Q_RANK, 2 * HEADS * LANES)),
                  _resident((D, 3 * LANES)), _resident((1, MLA_KV_RANK))]
                 + [pl.BlockSpec((tm, LANES), pos_map)] * 4,
        out_specs=[pl.BlockSpec((tm, HEADS * LANES), row), pl.BlockSpec((tm, LANES), row),
                   pl.BlockSpec((tm, LANES), row)],
        out_shape=[jax.ShapeDtypeStruct((rows, HEADS * LANES), BF16),
                   jax.ShapeDtypeStruct((rows, MLA_KV_RANK), F32),
                   jax.ShapeDtypeStruct((rows, LANES), F32)],
        compiler_params=_params(("parallel",)),
        name="mla_proj",
    )(x, g.reshape(1, D), wdq, gq.reshape(1, MLA_Q_RANK), wq, wkv, gkv.reshape(1, MLA_KV_RANK), *tables)


def _mla_expand_body(ckv_ref, kpe_ref, wexp_ref, wuv_ref, k_ref, v_ref):
    c = ckv_ref[...].astype(BF16)
    cat = jnp.concatenate([c, kpe_ref[...].astype(BF16)], axis=-1)
    k_ref[...] = _dot(cat, wexp_ref[...]).astype(BF16)
    v_ref[...] = _dot(c, wuv_ref[...]).astype(BF16)


def _mla_expand(ckv, kpe, wexp, wuv):
    rows = ckv.shape[0]
    tm = _row_tile(rows)
    row = lambda i: (i, 0)
    return pl.pallas_call(
        _mla_expand_body,
        grid=(rows // tm,),
        in_specs=[pl.BlockSpec((tm, LANES), row), pl.BlockSpec((tm, LANES), row),
                  _resident((2 * LANES, HEADS * LANES)), _resident((MLA_KV_RANK, D))],
        out_specs=[pl.BlockSpec((tm, HEADS * LANES), row), pl.BlockSpec((tm, D), row)],
        out_shape=[jax.ShapeDtypeStruct((rows, HEADS * LANES), BF16),
                   jax.ShapeDtypeStruct((rows, D), BF16)],
        compiler_params=_params(("parallel",)),
        name="mla_expand",
    )(ckv, kpe, wexp, wuv)


CONV_ROWS = 32
LANE_COLS = D // LANES


def _conv_body(x_ref, g_ref, w1_ref, b1_ref, hist_ref, dw_ref, bdw_ref, lng_ref, lnb_ref,
               a_ref, st_ref, gbuf, ybuf, *, tm):
    @pl.when(pl.program_id(1) == 0)
    def _():
        for c in range(LANE_COLS):
            gbuf[c, 0:HIST_PAD, :] = hist_ref[0, :, LANES * c:LANES * (c + 1)]

    h = _rms(x_ref[0], g_ref[...]).astype(BF16)
    u = _dot(h, w1_ref[...]) + b1_ref[...]
    glu = u[:, :D] * _sigmoid(u[:, D:])
    for c in range(LANE_COLS):
        gbuf[c, HIST_PAD:HIST_PAD + tm, :] = glu[:, LANES * c:LANES * (c + 1)]

    first = HIST_PAD - CONV_HIST

    def rows(i, _):
        r0 = pl.multiple_of(i * CONV_ROWS, CONV_ROWS)
        for c in range(LANE_COLS):
            cols = slice(LANES * c, LANES * (c + 1))
            acc = jnp.broadcast_to(bdw_ref[:, cols], (CONV_ROWS, LANES))
            for w in range(CONV_WIDTH):
                acc = acc + dw_ref[w:w + 1, cols] * gbuf[c, pl.ds(r0 + first + w, CONV_ROWS), :]
            ybuf[pl.ds(r0, CONV_ROWS), cols] = acc
        return 0

    lax.fori_loop(0, tm // CONV_ROWS, rows, 0)

    y = ybuf[...]
    yc = y - jnp.mean(y, axis=-1, keepdims=True)
    var = jnp.mean(yc * yc, axis=-1, keepdims=True)
    yn = yc * lax.rsqrt(var + EPS) * lng_ref[...] + lnb_ref[...]
    a_ref[0] = (yn * _sigmoid(yn)).astype(BF16)

    for c in range(LANE_COLS):
        tail = gbuf[c, tm:tm + HIST_PAD, :]
        st_ref[0, :, LANES * c:LANES * (c + 1)] = tail
        gbuf[c, 0:HIST_PAD, :] = tail


def _conv(x, g, w1, b1, hist, dw, bdw, lng, lnb):
    b, s, _ = x.shape
    tm = _row_tile(s)
    hist = jnp.pad(hist.astype(F32), ((0, 0), (HIST_PAD - CONV_HIST, 0), (0, 0)))
    dw = jnp.pad(dw.astype(F32), ((0, HIST_PAD - CONV_WIDTH), (0, 0)))
    a, st = pl.pallas_call(
        functools.partial(_conv_body, tm=tm),
        grid=(b, s // tm),
        in_specs=[pl.BlockSpec((1, tm, D), lambda bi, t: (bi, t, 0)), _resident((1, D)),
                  _resident((D, 2 * D)), _resident((1, 2 * D)),
                  pl.BlockSpec((1, HIST_PAD, D), lambda bi, t: (bi, 0, 0)),
                  _resident((HIST_PAD, D)), _resident((1, D)), _resident((1, D)), _resident((1, D))],
        out_specs=[pl.BlockSpec((1, tm, D), lambda bi, t: (bi, t, 0)),
                   pl.BlockSpec((1, HIST_PAD, D), lambda bi, t: (bi, 0, 0))],
        out_shape=[jax.ShapeDtypeStruct((b, s, D), BF16),
                   jax.ShapeDtypeStruct((b, HIST_PAD, D), F32)],
        scratch_shapes=[pltpu.VMEM((LANE_COLS, HIST_PAD + tm, LANES), F32), pltpu.VMEM((tm, D), F32)],
        compiler_params=_params(("parallel", "arbitrary")),
        name="conv_module",
    )(x, g.reshape(1, D), w1, b1.reshape(1, 2 * D), hist, dw, bdw.reshape(1, D),
      lng.reshape(1, D), lnb.reshape(1, D))
    return a, st[:, HIST_PAD - CONV_HIST:]


def _post_body(x_ref, a_ref, p_ref, wo_ref, bo_ref, gf_ref, w13_ref, w2_ref, gp_ref, wg_ref, wp_ref,
               gfin_ref, o_ref, h_s, *, final):
    x1 = x_ref[...] + _dot(a_ref[...], wo_ref[...]) + bo_ref[...]
    h_s[...] = _rms(x1, gf_ref[...]).astype(BF16)

    def gated(c):
        uv = _dot(h_s[...], w13_ref[c])
        u = uv[:, :FFN_CHUNK]
        return _dot((u * _sigmoid(u) * uv[:, FFN_CHUNK:]).astype(BF16), w2_ref[c])

    x2 = x1 + functools.reduce(jnp.add, [gated(c) for c in range(N_FFN_CHUNKS)])
    gate = _sigmoid(_dot(_rms(x2, gp_ref[...]).astype(BF16), wg_ref[...]))
    x3 = x2 + gate * _dot(p_ref[...].astype(BF16), wp_ref[...])
    o_ref[...] = _rms(x3, gfin_ref[...]) if final else x3


def _post(x, a, p, wo, bo, gf, w13, w2, gp, wg, wp, gfin, *, final):
    rows = x.shape[0]
    tm = _row_tile(rows)
    row = lambda i: (i, 0)
    return pl.pallas_call(
        functools.partial(_post_body, final=final),
        grid=(rows // tm,),
        in_specs=[pl.BlockSpec((tm, D), row), pl.BlockSpec((tm, D), row), pl.BlockSpec((tm, PLE_DIM), row),
                  _resident((D, D)), _resident((1, D)), _resident((1, D)),
                  _resident((N_FFN_CHUNKS, D, 2 * FFN_CHUNK)), _resident((N_FFN_CHUNKS, FFN_CHUNK, D)),
                  _resident((1, D)), _resident((D, D)), _resident((PLE_DIM, D)), _resident((1, D))],
        out_specs=pl.BlockSpec((tm, D), row),
        out_shape=jax.ShapeDtypeStruct((rows, D), F32),
        scratch_shapes=[pltpu.VMEM((tm, D), BF16)],
        compiler_params=_params(("parallel",)),
        name="post_mixer",
    )(x, a, p, wo, bo.reshape(1, D), gf.reshape(1, D), w13, w2, gp.reshape(1, D), wg, wp,
      gfin.reshape(1, D))


def _ffn_weights(w1, w3, w2):
    split = lambda w: w.reshape(D, N_FFN_CHUNKS, FFN_CHUNK).transpose(1, 0, 2)
    w13 = jnp.concatenate([split(w1), split(w3)], axis=-1).astype(BF16)
    return w13, w2.reshape(N_FFN_CHUNKS, FFN_CHUNK, D).astype(BF16)


def _with_cache(cache, new, total):
    b, past, w = cache.shape
    pad = jnp.zeros((b, total - past - new.shape[1], w), new.dtype)
    return jnp.concatenate([cache.astype(new.dtype), new, pad], axis=1)


def kernel(x_prompt, x_sample, p_prompt, p_sample, cache_sb_k, cache_sb_v, cache_mla_ckv, cache_mla_kpe, state_conv, cache_band_k, cache_band_v, norm_mix, norm_ffn, norm_ple, norm_final, ffn_w1, ffn_w3, ffn_w2, ple_proj, ple_gate, sb_wqkv, sb_wo, mla_wdq, mla_gq, mla_wuq, mla_wdkv, mla_gkv, mla_wuk, mla_wuv, mla_wo, conv_w1, conv_b1, conv_dw, conv_bdw, conv_ln_g, conv_ln_b, conv_w2, conv_b2, band_wqkv, band_rel_bias, band_wo):
    bp, s, _ = x_prompt.shape
    bs, t, _ = x_sample.shape
    depth = norm_mix.shape[0]
    past = cache_sb_k.shape[2]
    xp = x_prompt.reshape(bp * s, D)
    xs = x_sample.reshape(bs * t, D)
    pos_p = jnp.arange(s, dtype=jnp.int32)
    pos_s = past + jnp.arange(t, dtype=jnp.int32)
    zero_bias = jnp.zeros((D,), F32)
    head_scale = HEAD_DIM ** -0.5 * LOG2E
    sample_keys = -(-(past + t) // KEY_TILE) * KEY_TILE
    mla_keys = -(-(past + t) // SM_KEY_TILE) * SM_KEY_TILE

    outs = {name: [] for name in ("sb_kp", "sb_vp", "sb_ks", "sb_vs", "mla_cp", "mla_pp", "mla_cs", "mla_ps",
                                  "conv_p", "conv_s", "band_kp", "band_vp", "band_ks", "band_vs")}

    for i in range(depth):
        m, j = i % N_MIXERS, i // N_MIXERS
        if m == 0:
            w = sb_wqkv[j].astype(BF16)
            q, kf, vf, kb, vb = _qkv(xp, norm_mix[i], w, head_scale)
            ap = _sb_attn(q.reshape(bp, s, D), kb.reshape(bp, s, D), vb.reshape(bp, s, D),
                          tq=ROW_TILE, q_off=0)
            outs["sb_kp"].append(kf.reshape(bp, s, HEADS, HEAD_DIM))
            outs["sb_vp"].append(vf.reshape(bp, s, HEADS, HEAD_DIM))
            q, kf, vf, kb, vb = _qkv(xs, norm_mix[i], w, head_scale)
            k_all = _with_cache(cache_sb_k[j].reshape(bs, past, D), kb.reshape(bs, t, D), sample_keys)
            v_all = _with_cache(cache_sb_v[j].reshape(bs, past, D), vb.reshape(bs, t, D), sample_keys)
            as_ = _sb_attn(q.reshape(bs, t, D), k_all, v_all, tq=t, q_off=past // t)
            outs["sb_ks"].append(kf.reshape(bs, t, HEADS, HEAD_DIM))
            outs["sb_vs"].append(vf.reshape(bs, t, HEADS, HEAD_DIM))
            wo, bo = sb_wo[j].astype(BF16), zero_bias
        elif m == 1:
            wq, wkv, wexp = _mla_weights(mla_wuq[j], mla_wdkv[j], mla_wuk[j])
            wdq, wuv = mla_wdq[j].astype(BF16), mla_wuv[j].astype(BF16)
            q, ckv, kpe = _mla_proj(xp, norm_mix[i], wdq, mla_gq[j], wq, wkv, mla_gkv[j],
                                    _rope_tables(pos_p), s)
            k, v = _mla_expand(ckv, kpe, wexp, wuv)
            ap = _softmax_attn(q.reshape(bp, s, -1), k.reshape(bp, s, -1), v.reshape(bp, s, D),
                               tq=ROW_TILE, q_off=0)
            outs["mla_cp"].append(ckv.reshape(bp, s, MLA_KV_RANK))
            outs["mla_pp"].append(kpe[:, :MLA_ROPE].reshape(bp, s, MLA_ROPE))
            q, ckv, kpe = _mla_proj(xs, norm_mix[i], wdq, mla_gq[j], wq, wkv, mla_gkv[j],
                                    _rope_tables(pos_s), t)
            ckv_all = _with_cache(cache_mla_ckv[j], ckv.reshape(bs, t, MLA_KV_RANK), mla_keys)
            kpe_cache = jnp.pad(cache_mla_kpe[j], ((0, 0), (0, 0), (0, LANES - MLA_ROPE)))
            kpe_all = _with_cache(kpe_cache, kpe.reshape(bs, t, LANES), mla_keys)
            k, v = _mla_expand(ckv_all.reshape(bs * mla_keys, MLA_KV_RANK),
                               kpe_all.reshape(bs * mla_keys, LANES), wexp, wuv)
            as_ = _softmax_attn(q.reshape(bs, t, -1), k.reshape(bs, mla_keys, -1),
                                v.reshape(bs, mla_keys, D), tq=t, q_off=past // t)
            outs["mla_cs"].append(ckv.reshape(bs, t, MLA_KV_RANK))
            outs["mla_ps"].append(kpe[:, :MLA_ROPE].reshape(bs, t, MLA_ROPE))
            wo, bo = mla_wo[j].astype(BF16), zero_bias
        elif m == 2:
            w1 = conv_w1[j].astype(BF16)
            ap, st = _conv(xp.reshape(bp, s, D), norm_mix[i], w1, conv_b1[j],
                           jnp.zeros((bp, CONV_HIST, D), F32), conv_dw[j], conv_bdw[j],
                           conv_ln_g[j], conv_ln_b[j])
            outs["conv_p"].append(st)
            as_, st = _conv(xs.reshape(bs, t, D), norm_mix[i], w1, conv_b1[j], state_conv[j],
                            conv_dw[j], conv_bdw[j], conv_ln_g[j], conv_ln_b[j])
            outs["conv_s"].append(st)
            wo, bo = conv_w2[j].astype(BF16), conv_b2[j]
        else:
            w = band_wqkv[j].astype(BF16)
            bias_tiles = _band_bias_layouts(band_rel_bias[j])
            q, kf, vf, kb, vb = _qkv(xp, norm_mix[i], w, head_scale)
            ap = _band_attn(q.reshape(bp, s, D), kb.reshape(bp, s, D), vb.reshape(bp, s, D),
                            bias_tiles, q_off=0)
            keep = min(BAND_PAST, s)
            outs["band_kp"].append(kf.reshape(bp, s, HEADS, HEAD_DIM)[:, s - keep:])
            outs["band_vp"].append(vf.reshape(bp, s, HEADS, HEAD_DIM)[:, s - keep:])
            q, kf, vf, kb, vb = _qkv(xs, norm_mix[i], w, head_scale)
            n_past = cache_band_k.shape[2]
            local = n_past + BAND_TQ
            k_all = _with_cache(cache_band_k[j].reshape(bs, n_past, D), kb.reshape(bs, t, D), local)
            v_all = _with_cache(cache_band_v[j].reshape(bs, n_past, D), vb.reshape(bs, t, D), local)
            q_pad = jnp.pad(q.reshape(bs, t, D), ((0, 0), (0, BAND_TQ - t), (0, 0)))
            as_ = _band_attn(q_pad, k_all, v_all, bias_tiles, q_off=n_past // BAND_TQ)[:, :t]
            kk = jnp.concatenate([cache_band_k[j], kf.reshape(bs, t, HEADS, HEAD_DIM)], axis=1)
            vv = jnp.concatenate([cache_band_v[j], vf.reshape(bs, t, HEADS, HEAD_DIM)], axis=1)
            outs["band_ks"].append(kk[:, t:])
            outs["band_vs"].append(vv[:, t:])
            wo, bo = band_wo[j].astype(BF16), zero_bias

        w13, w2 = _ffn_weights(ffn_w1[i], ffn_w3[i], ffn_w2[i])
        final = i == depth - 1
        post = functools.partial(_post, wo=wo, bo=bo, gf=norm_ffn[i], w13=w13, w2=w2, gp=norm_ple[i],
                                 wg=ple_gate[i].astype(BF16), wp=ple_proj[i].astype(BF16),
                                 gfin=norm_final, final=final)
        xp = post(xp, ap.reshape(bp * s, D), p_prompt[i].reshape(bp * s, PLE_DIM))
        xs = post(xs, as_.reshape(bs * t, D), p_sample[i].reshape(bs * t, PLE_DIM))

    stack = lambda name: jnp.stack(outs[name])
    return (xp.reshape(bp, s, D), xs.reshape(bs, t, D),
            stack("sb_kp"), stack("sb_vp"), stack("sb_ks"), stack("sb_vs"),
            stack("mla_cp"), stack("mla_pp"), stack("mla_cs"), stack("mla_ps"),
            stack("conv_p"), stack("conv_s"),
            stack("band_kp"), stack("band_vp"), stack("band_ks"), stack("band_vs"))
```
